```python
import math
import jax, jax.numpy as jnp
from jax import lax
import numpy as np

D_MODEL = 2048
BATCH = 4
SEQ = 2048
DEPTH = 2

F32 = jnp.float32
N_META = 16
CHUNK = 64
LN_EPS = 1e-5
S5_WIDTH = D_MODEL // 2
S5_GROUP = 16
S5_GROUPS = S5_WIDTH // S5_GROUP
S5_STATE = 64
GLA_HEADS = 4
GLA_DK = D_MODEL // 16
GLA_DV = D_MODEL // 8
GLA_QK = GLA_HEADS * GLA_DK
GLA_WIDTH = GLA_HEADS * GLA_DV
GLA_RANK = 16
GLA_TAU = 16.0
RET_HEADS = 8
RET_DK = D_MODEL // RET_HEADS
RET_DV = 2 * RET_DK
RET_QK = RET_HEADS * RET_DK
RET_WIDTH = RET_HEADS * RET_DV
ROPE_BASE = 10000.0
AB_SPLITS = (S5_WIDTH, S5_WIDTH + GLA_QK, S5_WIDTH + 2 * GLA_QK, S5_WIDTH + 2 * GLA_QK + GLA_WIDTH,
             S5_WIDTH + 2 * GLA_QK + 2 * GLA_WIDTH)
AB_COLS = S5_WIDTH + 2 * GLA_QK + 2 * GLA_WIDTH + GLA_RANK
RET_SPLITS = (RET_QK, 2 * RET_QK, 2 * RET_QK + RET_WIDTH)
RET_COLS = 2 * RET_QK + 2 * RET_WIDTH
N_EXPERTS = 64
TOP_K = 8
N_GROUPS = 8
TOPK_GROUPS = 4
D_EXPERT = D_MODEL // 4
ROUTE_SCALE = 2.5
MOE_BLOCK = 128
DEEPNORM_ALPHA = (2 * DEPTH) ** 0.25
DEEPNORM_BETA = (8 * DEPTH) ** -0.25
N_EVEN = (DEPTH + 1) // 2
N_ODD = DEPTH // 2

kernel_name = "hybrid_s5_gla_retnet_moe_deepnorm"


def layer_norm(x, g, b):
    xf = x.astype(F32)
    mu = xf.mean(-1, keepdims=True)
    var = jnp.square(xf - mu).mean(-1, keepdims=True)
    return ((xf - mu) * lax.rsqrt(var + LN_EPS) * g.astype(F32) + b.astype(F32)).astype(x.dtype)


def head_norm(o, g):
    Bsz, L, H, dv = o.shape
    of = o.astype(F32)
    mu = of.mean(-1, keepdims=True)
    var = jnp.square(of - mu).mean(-1, keepdims=True)
    y = (of - mu) * lax.rsqrt(var + LN_EPS)
    return y.reshape(Bsz, L, H * dv) * g.astype(F32)


def to_chunks(t):
    Bsz, L, H, d = t.shape
    pad = (-L) % CHUNK
    t = jnp.pad(t, ((0, 0), (pad, 0), (0, 0), (0, 0)))
    return t.reshape(Bsz, (L + pad) // CHUNK, CHUNK, H, d).transpose(0, 3, 1, 2, 4)


def from_chunks(t, L):
    Bsz, H, N, C, d = t.shape
    t = t.transpose(0, 2, 3, 1, 4).reshape(Bsz, N * C, H, d)
    return t[:, N * C - L:]


def chunk_state_scan(q_in, k_in, v, state_decay):
    Bsz, H, N, C, dk = q_in.shape
    dv = v.shape[-1]

    def step(S, inp):
        qc, kc, vc, dc = inp
        o = jnp.einsum('bhcd,bhdv->bhcv', qc, S)
        S = S * dc[..., None] + jnp.einsum('bhcd,bhcv->bhdv', kc, vc)
        return S, o

    xs = tuple(jnp.moveaxis(t, 2, 0) for t in (q_in, k_in, v, state_decay))
    _, o = lax.scan(step, jnp.zeros((Bsz, H, dk, dv), F32), xs)
    return jnp.moveaxis(o, 0, 2)


def gla_chunked(q, k, v, g):
    b = jnp.cumsum(g, axis=3)
    b_last = b[:, :, :, -1:, :]
    q_in = q * jnp.exp(b)
    causal = jnp.tril(jnp.ones((CHUNK, CHUNK), F32))
    scores = jnp.einsum('bhnid,bhnjd->bhnij', q_in, k * jnp.exp(-b)) * causal
    o_intra = jnp.einsum('bhnij,bhnjv->bhniv', scores, v)
    o_inter = chunk_state_scan(q_in, k * jnp.exp(b_last - b), v, jnp.exp(b_last[:, :, :, 0, :]))
    return o_intra + o_inter


def retention_chunked(q, k, v):
    Bsz, H, N, C, dk = q.shape
    log_gamma = jnp.log(1.0 - 2.0 ** (-5.0 - jnp.arange(H, dtype=F32)))
    pos = jnp.arange(C, dtype=F32)
    diff = pos[:, None] - pos[None, :]
    dmask = jnp.where(diff >= 0, jnp.exp(log_gamma[:, None, None] * jnp.maximum(diff, 0.0)), 0.0)
    xi = jnp.exp(log_gamma[:, None] * (pos + 1.0))
    zeta = jnp.exp(log_gamma[:, None] * (C - 1.0 - pos))
    chunk_decay = jnp.exp(log_gamma * C)
    scores = jnp.einsum('bhnid,bhnjd->bhnij', q, k) * dmask[:, None]
    o_intra = jnp.einsum('bhnij,bhnjv->bhniv', scores, v)
    decay = jnp.broadcast_to(chunk_decay[None, :, None, None], (Bsz, H, N, dk))
    o_inter = chunk_state_scan(q * xi[:, None, :, None], k * zeta[:, None, :, None], v, decay)
    return o_intra + o_inter


def rotary(t):
    L, d = t.shape[1], t.shape[-1]
    inv = ROPE_BASE ** (-jnp.arange(0, d, 2, dtype=F32) / d)
    ang = jnp.arange(L, dtype=F32)[:, None] * inv[None, :]
    cos = jnp.cos(ang)[None, :, None, :]
    sin = jnp.sin(ang)[None, :, None, :]
    tf = t.astype(F32)
    t1, t2 = tf[..., : d // 2], tf[..., d // 2:]
    return jnp.concatenate([t1 * cos - t2 * sin, t1 * sin + t2 * cos], axis=-1)


def s5_mixer(u, log_dt, lam_re, lam_im, b_re, b_im, c_re, c_im, d, w_glu):
    Bsz, L, _ = u.shape
    uf = u.astype(F32)
    ug = uf.reshape(Bsz, L, S5_GROUPS, S5_GROUP)
    dt = jnp.exp(log_dt.astype(F32))[:, None]
    lr, li = lam_re.astype(F32), lam_im.astype(F32)
    mag = jnp.exp(lr * dt)
    ab_re, ab_im = mag * jnp.cos(li * dt), mag * jnp.sin(li * dt)
    den = lr * lr + li * li
    nr, ni = ab_re - 1.0, ab_im
    z_re = (nr * lr + ni * li) / den
    z_im = (ni * lr - nr * li) / den
    bu_re = jnp.einsum('blgc,gpc->blgp', ug, b_re.astype(F32))
    bu_im = jnp.einsum('blgc,gpc->blgp', ug, b_im.astype(F32))
    e_re = z_re * bu_re - z_im * bu_im
    e_im = z_re * bu_im + z_im * bu_re
    a_re = jnp.broadcast_to(ab_re, e_re.shape)
    a_im = jnp.broadcast_to(ab_im, e_im.shape)

    def combine(left, right):
        a1r, a1i, b1r, b1i = left
        a2r, a2i, b2r, b2i = right
        return (a2r * a1r - a2i * a1i, a2r * a1i + a2i * a1r,
                a2r * b1r - a2i * b1i + b2r, a2r * b1i + a2i * b1r + b2i)

    _, _, h_re, h_im = lax.associative_scan(combine, (a_re, a_im, e_re, e_im), axis=1)
    y = (jnp.einsum('blgp,gcp->blgc', h_re, c_re.astype(F32))
         - jnp.einsum('blgp,gcp->blgc', h_im, c_im.astype(F32)))
    y = y.reshape(Bsz, L, S5_WIDTH) + d.astype(F32) * uf
    z = jax.nn.gelu(y)
    return (z * jax.nn.sigmoid(z @ w_glu.astype(F32))).astype(u.dtype)


def ab_mixer(h, w_in, log_dt, lam_re, lam_im, b_re, b_im, c_re, c_im, d, w_glu,
             gla_w_gate2, gla_b_gate, gla_norm_g, w_out):
    Bsz, L, _ = h.shape
    proj = h @ w_in
    u, q, k, v, r, gl = jnp.split(proj, AB_SPLITS, axis=-1)
    y_a = s5_mixer(u, log_dt, lam_re, lam_im, b_re, b_im, c_re, c_im, d, w_glu)
    q = q.astype(F32).reshape(Bsz, L, GLA_HEADS, GLA_DK) * (GLA_DK ** -0.5)
    k = k.astype(F32).reshape(Bsz, L, GLA_HEADS, GLA_DK)
    v = v.astype(F32).reshape(Bsz, L, GLA_HEADS, GLA_DV)
    g = jax.nn.log_sigmoid((gl @ gla_w_gate2 + gla_b_gate).astype(F32)) / GLA_TAU
    g = g.reshape(Bsz, L, GLA_HEADS, GLA_DK)
    o = gla_chunked(to_chunks(q), to_chunks(k), to_chunks(v), to_chunks(g))
    o = head_norm(from_chunks(o, L), gla_norm_g) * jax.nn.silu(r.astype(F32))
    mixed = jnp.concatenate([y_a.astype(F32), o], axis=-1).astype(h.dtype)
    return mixed @ w_out


def retention_mixer(h, w_in, norm_g, w_out):
    Bsz, L, _ = h.shape
    proj = h @ w_in
    q, k, v, gt = jnp.split(proj, RET_SPLITS, axis=-1)
    q = rotary(q.reshape(Bsz, L, RET_HEADS, RET_DK))
    k = rotary(k.reshape(Bsz, L, RET_HEADS, RET_DK)) * (RET_DK ** -0.5)
    v = v.astype(F32).reshape(Bsz, L, RET_HEADS, RET_DV)
    o = retention_chunked(to_chunks(q), to_chunks(k), to_chunks(v))
    o = head_norm(from_chunks(o, L), norm_g) * jax.nn.silu(gt.astype(F32))
    return o.astype(h.dtype) @ w_out


def moe(h, router, bias, w_gate, w_up, w_down, s_gate, s_up, s_down):
    Bsz, L, D = h.shape
    T = Bsz * L
    xt = h.reshape(T, D)
    s = jax.nn.sigmoid((xt @ router).astype(F32))
    sb = s + bias.astype(F32)
    grp_score = lax.top_k(sb.reshape(T, N_GROUPS, N_EXPERTS // N_GROUPS), 2)[0].sum(-1)
    _, gidx = lax.top_k(grp_score, TOPK_GROUPS)
    gmask = (gidx[:, :, None] == jnp.arange(N_GROUPS)[None, None, :]).any(axis=1)
    emask = jnp.repeat(gmask, N_EXPERTS // N_GROUPS, axis=1)
    _, idx = lax.top_k(jnp.where(emask, sb, -jnp.inf), TOP_K)
    wsel = jnp.take_along_axis(s, idx, axis=1)
    wsel = wsel / wsel.sum(-1, keepdims=True) * ROUTE_SCALE
    n_assign = T * TOP_K
    n_blocks = -(-n_assign // MOE_BLOCK) + N_EXPERTS
    flat_e = idx.reshape(-1)
    flat_t = jnp.repeat(jnp.arange(T, dtype=jnp.int32), TOP_K)
    flat_w = wsel.reshape(-1)
    order = jnp.argsort(flat_e)
    se, st, sw = flat_e[order], flat_t[order], flat_w[order]
    counts = jnp.bincount(flat_e, length=N_EXPERTS)
    starts = jnp.cumsum(counts) - counts
    padded = ((counts + MOE_BLOCK - 1) // MOE_BLOCK) * MOE_BLOCK
    pend = jnp.cumsum(padded)
    dest = (pend - padded)[se] + (jnp.arange(n_assign) - starts[se])
    buf_tok = jnp.zeros((n_blocks * MOE_BLOCK,), jnp.int32).at[dest].set(st)
    buf_w = jnp.zeros((n_blocks * MOE_BLOCK,), F32).at[dest].set(sw)
    block_e = jnp.minimum(jnp.searchsorted(pend, jnp.arange(n_blocks) * MOE_BLOCK, side='right'), N_EXPERTS - 1)

    def step(acc, inp):
        tok, wb, e = inp
        xb = xt[tok]
        hb = jax.nn.silu(xb @ w_gate[e]) * (xb @ w_up[e])
        yb = (hb @ w_down[e]).astype(F32) * wb[:, None]
        return acc.at[tok].add(yb), None

    routed, _ = lax.scan(step, jnp.zeros((T, D), F32),
                         (buf_tok.reshape(n_blocks, MOE_BLOCK), buf_w.reshape(n_blocks, MOE_BLOCK), block_e))
    shared = (jax.nn.silu(xt @ s_gate) * (xt @ s_up)) @ s_down
    return (routed + shared.astype(F32)).astype(h.dtype).reshape(Bsz, L, D)


def setup_inputs(seed: int = 0) -> dict:
    key = jax.random.key(seed)
    ks = jax.random.split(key, 32)
    nrm = lambda k, shape, scale: jax.random.normal(k, shape, F32) * scale
    D, E, FF = D_MODEL, N_EXPERTS, D_EXPERT
    lam_im = (jnp.pi * jnp.arange(S5_STATE, dtype=F32))[None, None, :] + nrm(ks[4], (N_EVEN, S5_GROUPS, S5_STATE), 0.01)
    return {
        'x': nrm(ks[0], (BATCH, SEQ, D), 1.0),
        'meta': nrm(ks[1], (N_META, D), 1.0),
        'ab_w_in': nrm(ks[2], (N_EVEN, D, AB_COLS), D ** -0.5),
        's5_log_dt': jax.random.uniform(ks[3], (N_EVEN, S5_GROUPS), F32, math.log(0.001), math.log(0.1)),
        's5_lambda_re': -0.5 + nrm(ks[5], (N_EVEN, S5_GROUPS, S5_STATE), 0.01),
        's5_lambda_im': lam_im,
        's5_b_re': nrm(ks[6], (N_EVEN, S5_GROUPS, S5_STATE, S5_GROUP), (2 * S5_GROUP) ** -0.5),
        's5_b_im': nrm(ks[7], (N_EVEN, S5_GROUPS, S5_STATE, S5_GROUP), (2 * S5_GROUP) ** -0.5),
        's5_c_re': nrm(ks[8], (N_EVEN, S5_GROUPS, S5_GROUP, S5_STATE), S5_STATE ** -0.5),
        's5_c_im': nrm(ks[9], (N_EVEN, S5_GROUPS, S5_GROUP, S5_STATE), S5_STATE ** -0.5),
        's5_d': nrm(ks[10], (N_EVEN, S5_WIDTH), 1.0),
        's5_w_glu': nrm(ks[11], (N_EVEN, S5_WIDTH, S5_WIDTH), S5_WIDTH ** -0.5),
        'gla_w_gate2': nrm(ks[12], (N_EVEN, GLA_RANK, GLA_QK), GLA_RANK ** -0.5),
        'gla_b_gate': nrm(ks[13], (N_EVEN, GLA_QK), 0.1),
        'gla_norm_g': 1.0 + nrm(ks[14], (N_EVEN, GLA_WIDTH), 0.02),
        'ab_w_out': nrm(ks[15], (N_EVEN, S5_WIDTH + GLA_WIDTH, D), (S5_WIDTH + GLA_WIDTH) ** -0.5 * DEEPNORM_BETA),
        'ret_w_in': nrm(ks[16], (N_ODD, D, RET_COLS), D ** -0.5),
        'ret_norm_g': 1.0 + nrm(ks[17], (N_ODD, RET_WIDTH), 0.02),
        'ret_w_out': nrm(ks[18], (N_ODD, RET_WIDTH, D), RET_WIDTH ** -0.5 * DEEPNORM_BETA),
        'ln1_g': 1.0 + nrm(ks[19], (DEPTH, D), 0.02),
        'ln1_b': nrm(ks[20], (DEPTH, D), 0.02),
        'ln2_g': 1.0 + nrm(ks[21], (DEPTH, D), 0.02),
        'ln2_b': nrm(ks[22], (DEPTH, D), 0.02),
        'moe_router': nrm(ks[23], (DEPTH, D, E), D ** -0.5),
        'moe_bias': nrm(ks[24], (DEPTH, E), 0.01),
        'moe_w_gate': nrm(ks[25], (DEPTH, E, D, FF), D ** -0.5),
        'moe_w_up': nrm(ks[26], (DEPTH, E, D, FF), D ** -0.5),
        'moe_w_down': nrm(ks[27], (DEPTH, E, FF, D), FF ** -0.5 * DEEPNORM_BETA),
        'shared_w_gate': nrm(ks[28], (DEPTH, D, FF), D ** -0.5),
        'shared_w_up': nrm(ks[29], (DEPTH, D, FF), D ** -0.5),
        'shared_w_down': nrm(ks[30], (DEPTH, FF, D), FF ** -0.5 * DEEPNORM_BETA),
    }


def reference(x, meta, ab_w_in, s5_log_dt, s5_lambda_re, s5_lambda_im, s5_b_re, s5_b_im, s5_c_re, s5_c_im,
              s5_d, s5_w_glu, gla_w_gate2, gla_b_gate, gla_norm_g, ab_w_out, ret_w_in, ret_norm_g, ret_w_out,
              ln1_g, ln1_b, ln2_g, ln2_b, moe_router, moe_bias, moe_w_gate, moe_w_up, moe_w_down,
              shared_w_gate, shared_w_up, shared_w_down):
    Bsz = x.shape[0]
    h = jnp.concatenate([jnp.broadcast_to(meta[None].astype(x.dtype), (Bsz, N_META, D_MODEL)), x], axis=1)
    for layer in range(DEPTH):
        i = layer // 2
        if layer % 2 == 0:
            mix = ab_mixer(h, ab_w_in[i], s5_log_dt[i], s5_lambda_re[i], s5_lambda_im[i], s5_b_re[i], s5_b_im[i],
                           s5_c_re[i], s5_c_im[i], s5_d[i], s5_w_glu[i], gla_w_gate2[i], gla_b_gate[i],
                           gla_norm_g[i], ab_w_out[i])
        else:
            mix = retention_mixer(h, ret_w_in[i], ret_norm_g[i], ret_w_out[i])
        h = layer_norm(DEEPNORM_ALPHA * h + mix.astype(h.dtype), ln1_g[layer], ln1_b[layer])
        ffn = moe(h, moe_router[layer], moe_bias[layer], moe_w_gate[layer], moe_w_up[layer], moe_w_down[layer],
                  shared_w_gate[layer], shared_w_up[layer], shared_w_down[layer])
        h = layer_norm(DEEPNORM_ALPHA * h + ffn, ln2_g[layer], ln2_b[layer])
    return h[:, N_META:]
```

```python
import functools
import math

import jax
import jax.numpy as jnp
import numpy as np
from jax import lax
from jax.experimental import pallas as pl
from jax.experimental.pallas import tpu as pltpu

F32 = jnp.float32
BF16 = jnp.bfloat16
I32 = jnp.int32

N_META = 16
CHUNK = 64
LN_EPS = 1e-5
S5_GROUP = 16
S5_STATE = 64
S5_CHUNK = 16
GLA_HEADS = 4
GLA_RANK = 16
GLA_TAU = 16.0
RET_HEADS = 8
ROPE_BASE = 10000.0
N_EXPERTS = 64
TOP_K = 8
N_GROUPS = 8
TOPK_GROUPS = 4
ROUTE_SCALE = 2.5
DEPTH = 2
DEEPNORM_ALPHA = (2 * DEPTH) ** 0.25

LANES = 128
VMEM_LIMIT = 56 * 1024 * 1024


def _cparams(sem):
    return pltpu.CompilerParams(dimension_semantics=sem, vmem_limit_bytes=VMEM_LIMIT)


def _pick(n, cands):
    for c in cands:
        if n % c == 0:
            return c
    raise ValueError(f"no tile for {n} in {cands}")


def _dot(a, b):
    return jnp.dot(a, b, preferred_element_type=F32)


def _dot_nt(a, b):
    return lax.dot_general(a, b, (((1,), (1,)), ((), ())), preferred_element_type=F32)


def _dot_tn(a, b):
    return lax.dot_general(a, b, (((0,), (0,)), ((), ())), preferred_element_type=F32)


def _dot_hi(a, b):
    return jnp.dot(a, b, preferred_element_type=F32, precision=lax.Precision.HIGHEST)


def _mm_kernel(a_ref, w_ref, o_ref, acc_ref):
    k = pl.program_id(2)

    @pl.when(k == 0)
    def _():
        acc_ref[...] = jnp.zeros_like(acc_ref)

    acc_ref[...] += _dot(a_ref[...], w_ref[...])

    @pl.when(k == pl.num_programs(2) - 1)
    def _():
        o_ref[...] = acc_ref[...].astype(o_ref.dtype)


def _matmul(a, w, n_cols, out_dtype=F32):
    m, kdim = a.shape
    tm = _pick(m, (768, 512, 256, 128))
    tn = _pick(n_cols, (1024, 512, 256, 128))
    tk = _pick(kdim, (512, 256, 128))
    return pl.pallas_call(
        _mm_kernel,
        out_shape=jax.ShapeDtypeStruct((m, n_cols), out_dtype),
        grid=(m // tm, n_cols // tn, kdim // tk),
        in_specs=[pl.BlockSpec((tm, tk), lambda i, j, k: (i, k)),
                  pl.BlockSpec((tk, tn), lambda i, j, k: (k, j))],
        out_specs=pl.BlockSpec((tm, tn), lambda i, j, k: (i, j)),
        scratch_shapes=[pltpu.VMEM((tm, tn), F32)],
        compiler_params=_cparams(("parallel", "parallel", "arbitrary")),
        name="matmul",
    )(a, w)


def _layer_norm_rows(y, g, b):
    mu = jnp.mean(y, axis=-1, keepdims=True)
    var = jnp.mean(jnp.square(y - mu), axis=-1, keepdims=True)
    return (y - mu) * lax.rsqrt(var + LN_EPS) * g + b


def _mm_ln_kernel(a_ref, w_ref, h_ref, g_ref, b_ref, o_ref, ob_ref, acc_ref):
    k = pl.program_id(1)

    @pl.when(k == 0)
    def _():
        acc_ref[...] = jnp.zeros_like(acc_ref)

    acc_ref[...] += _dot(a_ref[...], w_ref[...])

    @pl.when(k == pl.num_programs(1) - 1)
    def _():
        y = DEEPNORM_ALPHA * h_ref[...] + acc_ref[...]
        out = _layer_norm_rows(y, g_ref[...], b_ref[...])
        o_ref[...] = out
        ob_ref[...] = out.astype(BF16)


def _matmul_res_ln(a, w, h, g, b):
    m, kdim = a.shape
    d = w.shape[1]
    tm = _pick(m, (256, 128))
    tk = _pick(kdim, (512, 256, 128))
    return pl.pallas_call(
        _mm_ln_kernel,
        out_shape=(jax.ShapeDtypeStruct((m, d), F32), jax.ShapeDtypeStruct((m, d), BF16)),
        grid=(m // tm, kdim // tk),
        in_specs=[pl.BlockSpec((tm, tk), lambda i, k: (i, k)),
                  pl.BlockSpec((tk, d), lambda i, k: (k, 0)),
                  pl.BlockSpec((tm, d), lambda i, k: (i, 0)),
                  pl.BlockSpec((1, d), lambda i, k: (0, 0)),
                  pl.BlockSpec((1, d), lambda i, k: (0, 0))],
        out_specs=(pl.BlockSpec((tm, d), lambda i, k: (i, 0)),
                   pl.BlockSpec((tm, d), lambda i, k: (i, 0))),
        scratch_shapes=[pltpu.VMEM((tm, d), F32)],
        compiler_params=_cparams(("parallel", "arbitrary")),
        name="matmul_res_ln",
    )(a, w, h, g.reshape(1, d), b.reshape(1, d))


def _log_sigmoid(x):
    return jnp.minimum(x, 0.0) - jnp.log1p(jnp.exp(-jnp.abs(x)))


def _gate_kernel(h_ref, wl_ref, w2_ref, b_ref, o_ref):
    low = _dot(h_ref[...], wl_ref[...])
    pre = _dot(low.astype(BF16), w2_ref[...]) + b_ref[...]
    o_ref[...] = _log_sigmoid(pre) / GLA_TAU


def _gla_gate(h_bf, w_low, w_gate2, b_gate):
    m, d = h_bf.shape
    qk = w_gate2.shape[1]
    tm = _pick(m, (256, 128))
    wl = jnp.zeros((d, LANES), BF16).at[:, :GLA_RANK].set(w_low.astype(BF16))
    w2 = jnp.zeros((LANES, qk), BF16).at[:GLA_RANK].set(w_gate2.astype(BF16))
    return pl.pallas_call(
        _gate_kernel,
        out_shape=jax.ShapeDtypeStruct((m, qk), F32),
        grid=(m // tm,),
        in_specs=[pl.BlockSpec((tm, d), lambda i: (i, 0)),
                  pl.BlockSpec((d, LANES), lambda i: (0, 0)),
                  pl.BlockSpec((LANES, qk), lambda i: (0, 0)),
                  pl.BlockSpec((1, qk), lambda i: (0, 0))],
        out_specs=pl.BlockSpec((tm, qk), lambda i: (i, 0)),
        compiler_params=_cparams(("parallel",)),
        name="gla_gate",
    )(h_bf, wl, w2, b_gate.reshape(1, qk))


def _head_norm_rows(o, gain):
    mu = jnp.mean(o, axis=-1, keepdims=True)
    var = jnp.mean(jnp.square(o - mu), axis=-1, keepdims=True)
    return (o - mu) * lax.rsqrt(var + LN_EPS) * gain


def _gla_kernel(q_ref, k_ref, v_ref, r_ref, g_ref, gain_ref, o_ref, s_ref, *, n_chunks, n_pad, dk):
    s_ref[...] = jnp.zeros_like(s_ref)
    row = lax.broadcasted_iota(I32, (CHUNK, CHUNK), 0)
    col = lax.broadcasted_iota(I32, (CHUNK, CHUNK), 1)
    tril = jnp.where(row >= col, 1.0, 0.0).astype(F32)
    first_valid = jnp.where(lax.broadcasted_iota(I32, (CHUNK, 1), 0) >= n_pad, 1.0, 0.0)
    gain = gain_ref[...]
    scale = dk ** -0.5

    def chunk(n, carry):
        sl = pl.ds(pl.multiple_of(n * CHUNK, CHUNK), CHUNK)
        valid = jnp.where(n == 0, first_valid, jnp.ones_like(first_valid))
        q = q_ref[sl, :] * scale * valid
        k = k_ref[sl, :] * valid
        v = v_ref[sl, :] * valid
        g = g_ref[sl, :] * valid
        b = _dot_hi(tril, g)
        b_last = b[CHUNK - 1:CHUNK, :]
        q_in = q * jnp.exp(b)
        k_out = k * jnp.exp(-b)
        k_end = k * jnp.exp(b_last - b)
        scores = _dot_nt(q_in.astype(BF16), k_out.astype(BF16)) * tril
        vb = v.astype(BF16)
        o = _dot(scores.astype(BF16), vb) + _dot_nt(q_in.astype(BF16), s_ref[...].astype(BF16))
        s_ref[...] = s_ref[...] * jnp.exp(b_last) + _dot_tn(vb, k_end.astype(BF16))
        r = r_ref[sl, :]
        o_ref[sl, :] = (_head_norm_rows(o, gain) * (r * jax.nn.sigmoid(r))).astype(o_ref.dtype)
        return carry

    lax.fori_loop(0, n_chunks, chunk, 0)


def _gla(proj3, gate3, norm_g, *, q_off, k_off, v_off, r_off, dk, dv):
    bsz, lp, _ = proj3.shape
    heads = GLA_HEADS
    kern = functools.partial(_gla_kernel, n_chunks=lp // CHUNK, n_pad=CHUNK - N_META, dk=dk)
    return pl.pallas_call(
        kern,
        out_shape=jax.ShapeDtypeStruct((bsz, lp, heads * dv), BF16),
        grid=(bsz, heads),
        in_specs=[pl.BlockSpec((None, lp, dk), lambda b, h: (b, 0, q_off // dk + h)),
                  pl.BlockSpec((None, lp, dk), lambda b, h: (b, 0, k_off // dk + h)),
                  pl.BlockSpec((None, lp, dv), lambda b, h: (b, 0, v_off // dv + h)),
                  pl.BlockSpec((None, lp, dv), lambda b, h: (b, 0, r_off // dv + h)),
                  pl.BlockSpec((None, lp, dk), lambda b, h: (b, 0, h)),
                  pl.BlockSpec((1, dv), lambda b, h: (0, h))],
        out_specs=pl.BlockSpec((None, lp, dv), lambda b, h: (b, 0, h)),
        scratch_shapes=[pltpu.VMEM((dv, dk), F32)],
        compiler_params=_cparams(("parallel", "parallel")),
        name="gla",
    )(proj3, proj3, proj3, proj3, gate3, norm_g.reshape(1, heads * dv))


def _ret_kernel(q_ref, k_ref, v_ref, gt_ref, cos_ref, sin_ref, dm_ref, xi_ref, zeta_ref, cd_ref, gain_ref,
                o_ref, s_ref, *, n_chunks, n_pad, dk):
    s_ref[...] = jnp.zeros_like(s_ref)
    first_valid = jnp.where(lax.broadcasted_iota(I32, (CHUNK, 1), 0) >= n_pad, 1.0, 0.0)
    gain = gain_ref[...]
    dmask = dm_ref[...]
    xi = xi_ref[...]
    zeta = zeta_ref[...]
    cd = cd_ref[...]
    scale = dk ** -0.5
    half = dk // 2

    def rot(t, cos, sin):
        t1, t2 = t[:, :half], t[:, half:]
        return jnp.concatenate([t1 * cos - t2 * sin, t1 * sin + t2 * cos], axis=-1)

    def chunk(n, carry):
        sl = pl.ds(pl.multiple_of(n * CHUNK, CHUNK), CHUNK)
        valid = jnp.where(n == 0, first_valid, jnp.ones_like(first_valid))
        cos, sin = cos_ref[sl, :], sin_ref[sl, :]
        q = rot(q_ref[sl, :], cos, sin) * valid
        k = rot(k_ref[sl, :], cos, sin) * (scale * valid)
        vb = (v_ref[sl, :] * valid).astype(BF16)
        scores = _dot_nt(q.astype(BF16), k.astype(BF16)) * dmask
        o = _dot(scores.astype(BF16), vb) + _dot((q * xi).astype(BF16), s_ref[...].astype(BF16))
        s_ref[...] = s_ref[...] * cd + _dot_tn((k * zeta).astype(BF16), vb)
        gt = gt_ref[sl, :]
        o_ref[sl, :] = (_head_norm_rows(o, gain) * (gt * jax.nn.sigmoid(gt))).astype(o_ref.dtype)
        return carry

    lax.fori_loop(0, n_chunks, chunk, 0)


def _retention(proj3, norm_g, *, dk, dv):
    bsz, lp, _ = proj3.shape
    heads = RET_HEADS
    half = dk // 2
    pos_tok = jnp.arange(lp, dtype=F32) - float(CHUNK - N_META)
    inv = ROPE_BASE ** (-jnp.arange(0, dk, 2, dtype=F32) / dk)
    ang = pos_tok[:, None] * inv[None, :]
    cos_t, sin_t = jnp.cos(ang), jnp.sin(ang)
    log_gamma = jnp.log(1.0 - 2.0 ** (-5.0 - jnp.arange(heads, dtype=F32)))
    pos = jnp.arange(CHUNK, dtype=F32)
    diff = pos[:, None] - pos[None, :]
    dmask = jnp.where(diff >= 0, jnp.exp(log_gamma[:, None, None] * jnp.maximum(diff, 0.0)), 0.0)
    xi = jnp.exp(log_gamma[:, None] * (pos + 1.0))[:, :, None]
    zeta = jnp.exp(log_gamma[:, None] * (CHUNK - 1.0 - pos))[:, :, None]
    cdec = jnp.exp(log_gamma * CHUNK)[:, None, None]
    kern = functools.partial(_ret_kernel, n_chunks=lp // CHUNK, n_pad=CHUNK - N_META, dk=dk)
    nq = heads
    nv = (2 * heads * dk) // dv
    return pl.pallas_call(
        kern,
        out_shape=jax.ShapeDtypeStruct((bsz, lp, heads * dv), BF16),
        grid=(bsz, heads),
        in_specs=[pl.BlockSpec((None, lp, dk), lambda b, h: (b, 0, h)),
                  pl.BlockSpec((None, lp, dk), lambda b, h: (b, 0, nq + h)),
                  pl.BlockSpec((None, lp, dv), lambda b, h: (b, 0, nv + h)),
                  pl.BlockSpec((None, lp, dv), lambda b, h: (b, 0, nv + heads + h)),
                  pl.BlockSpec((lp, half), lambda b, h: (0, 0)),
                  pl.BlockSpec((lp, half), lambda b, h: (0, 0)),
                  pl.BlockSpec((None, CHUNK, CHUNK), lambda b, h: (h, 0, 0)),
                  pl.BlockSpec((None, CHUNK, 1), lambda b, h: (h, 0, 0)),
                  pl.BlockSpec((None, CHUNK, 1), lambda b, h: (h, 0, 0)),
                  pl.BlockSpec((None, 1, 1), lambda b, h: (h, 0, 0)),
                  pl.BlockSpec((1, dv), lambda b, h: (0, h))],
        out_specs=pl.BlockSpec((None, lp, dv), lambda b, h: (b, 0, h)),
        scratch_shapes=[pltpu.VMEM((dk, dv), F32)],
        compiler_params=_cparams(("parallel", "parallel")),
        name="retention",
    )(proj3, proj3, proj3, proj3, cos_t, sin_t, dmask, xi, zeta, cdec, norm_g.reshape(1, heads * dv))


def _s5_prepare(log_dt, lam_re, lam_im, b_re, b_im, c_re, c_im, n_scan):
    cs = S5_CHUNK
    groups, states = lam_re.shape
    dt = jnp.exp(log_dt.astype(F32))[:, None]
    lr, li = lam_re.astype(F32), lam_im.astype(F32)
    mag = jnp.exp(lr * dt)
    ab_re, ab_im = mag * jnp.cos(li * dt), mag * jnp.sin(li * dt)
    den = lr * lr + li * li
    nr, ni = ab_re - 1.0, ab_im
    z_re = (nr * lr + ni * li) / den
    z_im = (ni * lr - nr * li) / den
    bz_re = z_re[..., None] * b_re - z_im[..., None] * b_im
    bz_im = z_re[..., None] * b_im + z_im[..., None] * b_re

    def power(kk):
        kk = kk.astype(F32)[..., None, None]
        m = jnp.exp(kk * (lr * dt))
        return m * jnp.cos(kk * (li * dt)), m * jnp.sin(kk * (li * dt))

    pw_re, pw_im = power(jnp.arange(cs + 1))
    cp_re = c_re[None] * pw_re[:, :, None, :] - c_im[None] * pw_im[:, :, None, :]
    cp_im = c_re[None] * pw_im[:, :, None, :] + c_im[None] * pw_re[:, :, None, :]
    hi = lax.Precision.HIGHEST
    kern = (jnp.einsum('tgcp,gpd->tgcd', cp_re, bz_re, precision=hi)
            - jnp.einsum('tgcp,gpd->tgcd', cp_im, bz_im, precision=hi))
    s_idx = jnp.arange(cs)[:, None]
    t_idx = jnp.arange(cs)[None, :]
    lag = jnp.clip(t_idx - s_idx, 0, cs - 1)
    toep = jnp.where((t_idx >= s_idx)[:, :, None, None, None], kern[lag], 0.0)
    toep = toep.transpose(2, 0, 4, 1, 3).reshape(groups, cs * S5_GROUP, cs * S5_GROUP)
    rev = pw_re[cs - 1 - jnp.arange(cs)], pw_im[cs - 1 - jnp.arange(cs)]
    win_re = rev[0][..., None] * bz_re[None] - rev[1][..., None] * bz_im[None]
    win_im = rev[0][..., None] * bz_im[None] + rev[1][..., None] * bz_re[None]
    win = jnp.concatenate([win_re, win_im], axis=2)
    win = win.transpose(1, 0, 3, 2).reshape(groups, cs * S5_GROUP, 2 * states)
    wout = jnp.concatenate([cp_re[1:], -cp_im[1:]], axis=3)
    wout = wout.transpose(1, 3, 0, 2).reshape(groups, 2 * states, cs * S5_GROUP)
    sc_re, sc_im = power(cs * (2 ** jnp.arange(n_scan)))
    mul_r = jnp.concatenate([sc_re, sc_re], axis=-1)
    mul_i = jnp.concatenate([-sc_im, sc_im], axis=-1)
    scan_mul = jnp.stack([mul_r, mul_i], axis=2).transpose(1, 0, 2, 3)
    return toep.astype(BF16), win.astype(BF16), wout.astype(BF16), scan_mul


def _s5_kernel(x_ref, toep_ref, win_ref, wout_ref, mul_ref, y_ref, *, bsz, n_steps, n_scan, n_pad_steps):
    rows = bsz * n_steps
    states2 = win_ref.shape[-1]
    step = lax.broadcasted_iota(I32, (rows, 1), 0)
    for bb in range(1, bsz):
        step = step - jnp.where(lax.broadcasted_iota(I32, (rows, 1), 0) >= bb * n_steps, n_steps, 0)
    x = jnp.where(step >= n_pad_steps, x_ref[...], 0.0).astype(BF16)
    y = _dot(x, toep_ref[...])
    acc = _dot(x, win_ref[...])
    for j in range(n_scan):
        sh = 2 ** j
        prev = jnp.where(step >= sh, pltpu.roll(acc, sh, 0), 0.0)
        swapped = pltpu.roll(prev, states2 // 2, 1)
        acc = acc + mul_ref[j, 0:1, :] * prev + mul_ref[j, 1:2, :] * swapped
    start = jnp.where(step >= 1, pltpu.roll(acc, 1, 0), 0.0)
    y_ref[...] = y + _dot(start.astype(BF16), wout_ref[...])


def _s5_scan(x, toep, win, wout, scan_mul, *, bsz, n_steps, n_pad_steps):
    groups, rows, width = x.shape
    states2 = win.shape[-1]
    n_scan = scan_mul.shape[1]
    kern = functools.partial(_s5_kernel, bsz=bsz, n_steps=n_steps, n_scan=n_scan, n_pad_steps=n_pad_steps)
    return pl.pallas_call(
        kern,
        out_shape=jax.ShapeDtypeStruct((groups, rows, width), F32),
        grid=(groups,),
        in_specs=[pl.BlockSpec((None, rows, width), lambda g: (g, 0, 0)),
                  pl.BlockSpec((None, width, width), lambda g: (g, 0, 0)),
                  pl.BlockSpec((None, width, states2), lambda g: (g, 0, 0)),
                  pl.BlockSpec((None, states2, width), lambda g: (g, 0, 0)),
                  pl.BlockSpec((None, n_scan, 2, states2), lambda g: (g, 0, 0, 0))],
        out_specs=pl.BlockSpec((None, rows, width), lambda g: (g, 0, 0)),
        compiler_params=_cparams(("parallel",)),
        name="s5_scan",
    )(x, toep, win, wout, scan_mul)


def _glu_kernel(y_ref, u_ref, d_ref, w_ref, o_ref):
    z = jax.nn.gelu(y_ref[...] + d_ref[...] * u_ref[...])
    o_ref[...] = (z * jax.nn.sigmoid(_dot(z.astype(BF16), w_ref[...]))).astype(o_ref.dtype)


def _s5_glu(y, proj, d, w_glu_bf):
    m, width = y.shape
    tm = _pick(m, (256, 128))
    return pl.pallas_call(
        _glu_kernel,
        out_shape=jax.ShapeDtypeStruct((m, width), BF16),
        grid=(m // tm,),
        in_specs=[pl.BlockSpec((tm, width), lambda i: (i, 0)),
                  pl.BlockSpec((tm, width), lambda i: (i, 0)),
                  pl.BlockSpec((1, width), lambda i: (0, 0)),
                  pl.BlockSpec((width, width), lambda i: (0, 0))],
        out_specs=pl.BlockSpec((tm, width), lambda i: (i, 0)),
        compiler_params=_cparams(("parallel",)),
        name="s5_glu",
    )(y, proj, d.reshape(1, width), w_glu_bf)


def _s5_mixer(proj, bsz, lp, params):
    log_dt, lam_re, lam_im, b_re, b_im, c_re, c_im, d, w_glu = params
    groups = lam_re.shape[0]
    width = groups * S5_GROUP
    cs = S5_CHUNK
    n_real = lp // cs
    n_steps = -(-n_real // 8) * 8
    n_scan = max(1, math.ceil(math.log2(n_steps)))
    toep, win, wout, scan_mul = _s5_prepare(log_dt, lam_re, lam_im, b_re, b_im, c_re, c_im, n_scan)
    u = proj[:, :width].reshape(bsz, n_real, cs, groups, S5_GROUP)
    x = u.transpose(3, 0, 1, 2, 4).reshape(groups, bsz, n_real, cs * S5_GROUP)
    x = jnp.pad(x, ((0, 0), (0, 0), (0, n_steps - n_real), (0, 0))).reshape(groups, bsz * n_steps, cs * S5_GROUP)
    y = _s5_scan(x, toep, win, wout, scan_mul, bsz=bsz, n_steps=n_steps,
                 n_pad_steps=(CHUNK - N_META) // cs)
    y = y.reshape(groups, bsz, n_steps, cs, S5_GROUP)[:, :, :n_real]
    y = y.transpose(1, 2, 3, 0, 4).reshape(bsz * lp, width)
    return _s5_glu(y, proj, d, w_glu.astype(BF16))


def _router_kernel(x_ref, rt_ref, bias_ref, idx_ref, pos_ref, w_ref, cnt_ref, run_ref, *, tm):
    i = pl.program_id(0)

    @pl.when(i == 0)
    def _():
        run_ref[...] = jnp.zeros_like(run_ref)

    ne, ng = N_EXPERTS, N_GROUPS
    per = ne // ng
    neg = -jnp.inf
    logits = lax.dot_general(rt_ref[...], x_ref[...], (((1,), (1,)), ((), ())),
                             preferred_element_type=F32, precision=lax.Precision.HIGHEST)
    s = jax.nn.sigmoid(logits)
    sb = s + bias_ref[...]
    sb3 = sb.reshape(ng, per, tm)
    io = lax.broadcasted_iota(I32, (ng, per, tm), 1)
    m1 = jnp.max(sb3, axis=1, keepdims=True)
    first = jnp.min(jnp.where(sb3 == m1, io, per), axis=1, keepdims=True)
    m2 = jnp.max(jnp.where(io == first, neg, sb3), axis=1, keepdims=True)
    gs = m1 + m2
    gi = lax.broadcasted_iota(I32, (ng, 1, tm), 0)
    grank = jnp.zeros((ng, 1, tm), F32)
    for g2 in range(ng):
        other = gs[g2:g2 + 1]
        beats = (other > gs) | ((other == gs) & (g2 < gi))
        grank = grank + jnp.where(beats, 1.0, 0.0)
    gsel = grank < float(TOPK_GROUPS)
    ms = jnp.where(gsel, sb3, neg).reshape(ne, tm)
    ei = lax.broadcasted_iota(I32, (ne, tm), 0)
    rank = jnp.zeros((ne, tm), F32)
    for e2 in range(ne):
        other = ms[e2:e2 + 1, :]
        beats = (other > ms) | ((other == ms) & (e2 < ei))
        rank = rank + jnp.where(beats, 1.0, 0.0)
    sel = rank < float(TOP_K)
    self_ = jnp.where(sel, 1.0, 0.0)
    wsum = jnp.sum(jnp.where(sel, s, 0.0), axis=0, keepdims=True)
    wgt = s / wsum * ROUTE_SCALE
    ti = lax.broadcasted_iota(I32, (tm, tm), 0)
    tj = lax.broadcasted_iota(I32, (tm, tm), 1)
    upper = jnp.where(ti < tj, 1.0, 0.0).astype(BF16)
    pos = _dot(self_.astype(BF16), upper) + run_ref[:, 0:1]
    run_ref[...] = run_ref[...] + jnp.sum(self_, axis=1, keepdims=True)
    li = lax.broadcasted_iota(I32, (ne, ne), 0)
    lj = lax.broadcasted_iota(I32, (ne, ne), 1)
    lower = jnp.where(lj < li, 1.0, 0.0).astype(BF16)
    slot = _dot(lower, self_.astype(BF16))
    eif = ei.astype(F32)
    idx_rows, pos_rows, w_rows = [], [], []
    for kk in range(TOP_K):
        one = sel & (slot == float(kk))
        idx_rows.append(jnp.sum(jnp.where(one, eif, 0.0), axis=0, keepdims=True))
        pos_rows.append(jnp.sum(jnp.where(one, pos, 0.0), axis=0, keepdims=True))
        w_rows.append(jnp.sum(jnp.where(one, wgt, 0.0), axis=0, keepdims=True))
    idx_ref[...] = jnp.concatenate(idx_rows, axis=0).astype(I32)
    pos_ref[...] = jnp.concatenate(pos_rows, axis=0).astype(I32)
    w_ref[...] = jnp.concatenate(w_rows, axis=0)

    @pl.when(i == pl.num_programs(0) - 1)
    def _():
        cnt_ref[...] = run_ref[...].astype(I32)


def _router(h, router, bias):
    m, d = h.shape
    ne = router.shape[1]
    tm = _pick(m, (256, 128))
    kern = functools.partial(_router_kernel, tm=tm)
    return pl.pallas_call(
        kern,
        out_shape=(jax.ShapeDtypeStruct((TOP_K, m), I32), jax.ShapeDtypeStruct((TOP_K, m), I32),
                   jax.ShapeDtypeStruct((TOP_K, m), F32), jax.ShapeDtypeStruct((ne, LANES), I32)),
        grid=(m // tm,),
        in_specs=[pl.BlockSpec((tm, d), lambda i: (i, 0)),
                  pl.BlockSpec((ne, d), lambda i: (0, 0)),
                  pl.BlockSpec((ne, 1), lambda i: (0, 0))],
        out_specs=(pl.BlockSpec((TOP_K, tm), lambda i: (0, i)),
                   pl.BlockSpec((TOP_K, tm), lambda i: (0, i)),
                   pl.BlockSpec((TOP_K, tm), lambda i: (0, i)),
                   pl.BlockSpec((ne, LANES), lambda i: (0, 0))),
        scratch_shapes=[pltpu.VMEM((ne, LANES), F32)],
        compiler_params=_cparams(("arbitrary",)),
        name="moe_router",
    )(h, router.T, bias.reshape(ne, 1))


def _dispatch_kernel(dest_ref, x_ref, init_ref, xs_ref, sem, *, tm):
    del init_ref
    i = pl.program_id(0)

    def copy(r, kk):
        return pltpu.make_async_copy(x_ref.at[pl.ds(i * tm + r, 1)],
                                     xs_ref.at[pl.ds(dest_ref[kk, r], 1)], sem)

    def issue(r, c):
        for kk in range(TOP_K):
            copy(r, kk).start()
        return c

    def drain(r, c):
        for kk in range(TOP_K):
            copy(r, kk).wait()
        return c

    lax.fori_loop(0, tm, issue, 0)
    lax.fori_loop(0, tm, drain, 0)


def _dispatch(x, dest, n_rows):
    m, d = x.shape
    tm = _pick(m, (256, 128))
    kern = functools.partial(_dispatch_kernel, tm=tm)
    return pl.pallas_call(
        kern,
        out_shape=jax.ShapeDtypeStruct((n_rows, d), x.dtype),
        grid=(m // tm,),
        in_specs=[pl.BlockSpec((TOP_K, tm), lambda i: (0, i), memory_space=pltpu.SMEM),
                  pl.BlockSpec(memory_space=pl.ANY),
                  pl.BlockSpec(memory_space=pl.ANY)],
        out_specs=pl.BlockSpec(memory_space=pl.ANY),
        scratch_shapes=[pltpu.SemaphoreType.DMA],
        input_output_aliases={2: 0},
        compiler_params=_cparams(("arbitrary",)),
        name="moe_dispatch",
    )(dest, x, jnp.zeros((n_rows, d), x.dtype))


def _ffn_kernel(be_ref, nb_ref, x_ref, wg_ref, wu_ref, wd_ref, y_ref, wgb_ref, wub_ref, wdb_ref):
    i = pl.program_id(0)
    prev = be_ref[jnp.maximum(i - 1, 0)]
    fresh = (i == 0) | (be_ref[i] != prev)

    @pl.when(fresh & (i < nb_ref[0]))
    def _():
        wgb_ref[...] = wg_ref[...].astype(BF16)
        wub_ref[...] = wu_ref[...].astype(BF16)
        wdb_ref[...] = wd_ref[...].astype(BF16)

    @pl.when(i < nb_ref[0])
    def _():
        x = x_ref[...].astype(BF16)
        a = _dot(x, wgb_ref[...])
        u = _dot(x, wub_ref[...])
        hmid = (a * jax.nn.sigmoid(a)) * u
        y_ref[...] = _dot(hmid.astype(BF16), wdb_ref[...]).astype(y_ref.dtype)

    @pl.when(i >= nb_ref[0])
    def _():
        y_ref[...] = jnp.zeros_like(y_ref)


def _ffn(xs, block_e, n_active, w_gate, w_up, w_down, tm, out_dtype=F32):
    n_rows, d = xs.shape
    ff = w_gate.shape[-1]
    nb = n_rows // tm

    def xmap(i, be, nbr):
        return (jnp.minimum(i, nbr[0] - 1), 0)

    def wmap(i, be, nbr):
        return (be[i], 0, 0)

    return pl.pallas_call(
        _ffn_kernel,
        out_shape=jax.ShapeDtypeStruct((n_rows, d), out_dtype),
        grid_spec=pltpu.PrefetchScalarGridSpec(
            num_scalar_prefetch=2,
            grid=(nb,),
            in_specs=[pl.BlockSpec((tm, d), xmap),
                      pl.BlockSpec((None, d, ff), wmap),
                      pl.BlockSpec((None, d, ff), wmap),
                      pl.BlockSpec((None, ff, d), wmap)],
            out_specs=pl.BlockSpec((tm, d), lambda i, be, nbr: (i, 0)),
            scratch_shapes=[pltpu.VMEM((d, ff), BF16), pltpu.VMEM((d, ff), BF16), pltpu.VMEM((ff, d), BF16)]),
        compiler_params=_cparams(("arbitrary",)),
        name="moe_ffn",
    )(block_e, n_active, xs, w_gate, w_up, w_down)


def _combine_kernel(dest_ref, ys_ref, w_ref, sh_ref, h_ref, g_ref, b_ref, o_ref, ob_ref, buf_ref, sem, *, tm):
    def copy(r, kk):
        return pltpu.make_async_copy(ys_ref.at[pl.ds(dest_ref[kk, r], 1)],
                                     buf_ref.at[kk, pl.ds(r, 1)], sem)

    def issue(r, c):
        for kk in range(TOP_K):
            copy(r, kk).start()
        return c

    def drain(r, c):
        for kk in range(TOP_K):
            copy(r, kk).wait()
        return c

    lax.fori_loop(0, tm, issue, 0)
    lax.fori_loop(0, tm, drain, 0)
    w = w_ref[...]
    routed = w[:, 0:1] * buf_ref[0]
    for kk in range(1, TOP_K):
        routed = routed + w[:, kk:kk + 1] * buf_ref[kk]
    y = DEEPNORM_ALPHA * h_ref[...] + (routed + sh_ref[...])
    out = _layer_norm_rows(y, g_ref[...], b_ref[...])
    o_ref[...] = out
    ob_ref[...] = out.astype(BF16)


def _combine_ln(ys, dest, w_t, shared, h, g, b):
    m, d = h.shape
    tm = _pick(m, (128,))
    kern = functools.partial(_combine_kernel, tm=tm)
    return pl.pallas_call(
        kern,
        out_shape=(jax.ShapeDtypeStruct((m, d), F32), jax.ShapeDtypeStruct((m, d), BF16)),
        grid=(m // tm,),
        in_specs=[pl.BlockSpec((TOP_K, tm), lambda i: (0, i), memory_space=pltpu.SMEM),
                  pl.BlockSpec(memory_space=pl.ANY),
                  pl.BlockSpec((tm, TOP_K), lambda i: (i, 0)),
                  pl.BlockSpec((tm, d), lambda i: (i, 0)),
                  pl.BlockSpec((tm, d), lambda i: (i, 0)),
                  pl.BlockSpec((1, d), lambda i: (0, 0)),
                  pl.BlockSpec((1, d), lambda i: (0, 0))],
        out_specs=(pl.BlockSpec((tm, d), lambda i: (i, 0)),
                   pl.BlockSpec((tm, d), lambda i: (i, 0))),
        scratch_shapes=[pltpu.VMEM((TOP_K, tm, d), F32), pltpu.SemaphoreType.DMA],
        compiler_params=_cparams(("arbitrary",)),
        name="moe_combine_ln",
    )(dest, ys, w_t, shared, h, g.reshape(1, d), b.reshape(1, d))


def _moe_ln(h, h_bf, router, bias, w_gate, w_up, w_down, s_gate, s_up, s_down, ln_g, ln_b):
    m, d = h.shape
    tm = _pick(m, (256, 128))
    idx, pos, wsel, counts = _router(h, router, bias)
    counts = counts[:, 0]
    padded = ((counts + tm - 1) // tm) * tm
    pend = jnp.cumsum(padded)
    dest = (pend - padded)[idx] + pos
    nb = (m * TOP_K) // tm + N_EXPERTS
    block_e = jnp.minimum(jnp.searchsorted(pend, jnp.arange(nb, dtype=I32) * tm, side='right'),
                          N_EXPERTS - 1).astype(I32)
    n_active = (pend[-1:] // tm).astype(I32)
    xs = _dispatch(h, dest, nb * tm)
    ys = _ffn(xs, block_e, n_active, w_gate, w_up, w_down, tm)
    shared = _ffn(h_bf, jnp.zeros((m // tm,), I32), jnp.full((1,), m // tm, I32),
                  s_gate[None], s_up[None], s_down[None], tm)
    return _combine_ln(ys, dest, wsel.T, shared, h, ln_g, ln_b)


def kernel(x, meta, ab_w_in, s5_log_dt, s5_lambda_re, s5_lambda_im, s5_b_re, s5_b_im, s5_c_re, s5_c_im, s5_d, s5_w_glu, gla_w_gate2, gla_b_gate, gla_norm_g, ab_w_out, ret_w_in, ret_norm_g, ret_w_out, ln1_g, ln1_b, ln2_g, ln2_b, moe_router, moe_bias, moe_w_gate, moe_w_up, moe_w_down, shared_w_gate, shared_w_up, shared_w_down):
    bsz, seq, d = x.shape
    length = seq + N_META
    pad = (-length) % CHUNK
    assert pad == CHUNK - N_META, "sequence length must be a multiple of the mixer chunk"
    lp = length + pad
    m = bsz * lp
    h3 = jnp.concatenate([jnp.zeros((bsz, pad, d), x.dtype),
                          jnp.broadcast_to(meta[None].astype(x.dtype), (bsz, N_META, d)), x], axis=1)
    h = h3.reshape(m, d)
    h_bf = h.astype(BF16)
    depth = ln1_g.shape[0]
    for layer in range(depth):
        i = layer // 2
        if layer % 2 == 0:
            s5_w = s5_lambda_re.shape[1] * S5_GROUP
            gla_qk = gla_w_gate2.shape[2]
            gla_w = gla_norm_g.shape[1]
            n_main = s5_w + 2 * gla_qk + 2 * gla_w
            w_in = ab_w_in[i].astype(BF16)
            proj = _matmul(h_bf, w_in, n_main)
            gate = _gla_gate(h_bf, ab_w_in[i][:, n_main:], gla_w_gate2[i], gla_b_gate[i])
            y_a = _s5_mixer(proj, bsz, lp, (s5_log_dt[i], s5_lambda_re[i], s5_lambda_im[i], s5_b_re[i], s5_b_im[i],
                                            s5_c_re[i], s5_c_im[i], s5_d[i], s5_w_glu[i]))
            o = _gla(proj.reshape(bsz, lp, n_main), gate.reshape(bsz, lp, gla_qk), gla_norm_g[i],
                     q_off=s5_w, k_off=s5_w + gla_qk, v_off=s5_w + 2 * gla_qk, r_off=s5_w + 2 * gla_qk + gla_w,
                     dk=gla_qk // GLA_HEADS, dv=gla_w // GLA_HEADS)
            mixed = jnp.concatenate([y_a, o.reshape(m, gla_w)], axis=1)
            w_out = ab_w_out[i].astype(BF16)
        else:
            ret_w = ret_norm_g.shape[1]
            ret_qk = (ret_w_in.shape[2] - 2 * ret_w) // 2
            proj = _matmul(h_bf, ret_w_in[i].astype(BF16), ret_w_in.shape[2])
            o = _retention(proj.reshape(bsz, lp, ret_w_in.shape[2]), ret_norm_g[i],
                           dk=ret_qk // RET_HEADS, dv=ret_w // RET_HEADS)
            mixed = o.reshape(m, ret_w)
            w_out = ret_w_out[i].astype(BF16)
        h, h_bf = _matmul_res_ln(mixed, w_out, h, ln1_g[layer], ln1_b[layer])
        h, h_bf = _moe_ln(h, h_bf, moe_router[layer], moe_bias[layer], moe_w_gate[layer], moe_w_up[layer],
                          moe_w_down[layer], shared_w_gate[layer], shared_w_up[layer], shared_w_down[layer],
                          ln2_g[layer], ln2_b[layer])
    return h.reshape(bsz, lp, d)[:, pad + N_META:]
```

```python
import functools
import math

import jax
import jax.numpy as jnp
import numpy as np
from jax import lax
from jax.experimental import pallas as pl
from jax.experimental.pallas import tpu as pltpu

F32 = jnp.float32
BF16 = jnp.bfloat16
I32 = jnp.int32

N_META = 16
CHUNK = 64
LN_EPS = 1e-5
S5_GROUP = 16
S5_STATE = 64
S5_CHUNK = 16
GLA_HEADS = 4
GLA_RANK = 16
GLA_TAU = 16.0
RET_HEADS = 8
ROPE_BASE = 10000.0
N_EXPERTS = 64
TOP_K = 8
N_GROUPS = 8
TOPK_GROUPS = 4
ROUTE_SCALE = 2.5
DEPTH = 2
DEEPNORM_ALPHA = (2 * DEPTH) ** 0.25

LANES = 128
VMEM_LIMIT = 56 * 1024 * 1024


def _cparams(sem):
    return pltpu.CompilerParams(dimension_semantics=sem, vmem_limit_bytes=VMEM_LIMIT)


def _pick(n, cands):
    for c in cands:
        if n % c == 0:
            return c
    raise ValueError(f"no tile for {n} in {cands}")


def _dot(a, b):
    return jnp.dot(a, b, preferred_element_type=F32)


def _dot_nt(a, b):
    return lax.dot_general(a, b, (((1,), (1,)), ((), ())), preferred_element_type=F32)


def _dot_tn(a, b):
    return lax.dot_general(a, b, (((0,), (0,)), ((), ())), preferred_element_type=F32)


def _dot_hi(a, b):
    return jnp.dot(a, b, preferred_element_type=F32, precision=lax.Precision.HIGHEST)


def _mm_kernel(a_ref, w_ref, o_ref, acc_ref):
    k = pl.program_id(2)

    @pl.when(k == 0)
    def _():
        acc_ref[...] = jnp.zeros_like(acc_ref)

    acc_ref[...] += _dot(a_ref[...], w_ref[...])

    @pl.when(k == pl.num_programs(2) - 1)
    def _():
        o_ref[...] = acc_ref[...].astype(o_ref.dtype)


def _matmul(a, w, n_cols, out_dtype=F32):
    m, kdim = a.shape
    tm = _pick(m, (768, 512, 256, 128))
    tn = _pick(n_cols, (1024, 512, 256, 128))
    tk = _pick(kdim, (512, 256, 128))
    return pl.pallas_call(
        _mm_kernel,
        out_shape=jax.ShapeDtypeStruct((m, n_cols), out_dtype),
        grid=(m // tm, n_cols // tn, kdim // tk),
        in_specs=[pl.BlockSpec((tm, tk), lambda i, j, k: (i, k)),
                  pl.BlockSpec((tk, tn), lambda i, j, k: (k, j))],
        out_specs=pl.BlockSpec((tm, tn), lambda i, j, k: (i, j)),
        scratch_shapes=[pltpu.VMEM((tm, tn), F32)],
        compiler_params=_cparams(("parallel", "parallel", "arbitrary")),
        name="matmul",
    )(a, w)


def _layer_norm_rows(y, g, b):
    mu = jnp.mean(y, axis=-1, keepdims=True)
    var = jnp.mean(jnp.square(y - mu), axis=-1, keepdims=True)
    return (y - mu) * lax.rsqrt(var + LN_EPS) * g + b


def _mm_ln_kernel(a_ref, w_ref, h_ref, g_ref, b_ref, o_ref, ob_ref, acc_ref):
    k = pl.program_id(1)

    @pl.when(k == 0)
    def _():
        acc_ref[...] = jnp.zeros_like(acc_ref)

    acc_ref[...] += _dot(a_ref[...], w_ref[...])

    @pl.when(k == pl.num_programs(1) - 1)
    def _():
        y = DEEPNORM_ALPHA * h_ref[...] + acc_ref[...]
        out = _layer_norm_rows(y, g_ref[...], b_ref[...])
        o_ref[...] = out
        ob_ref[...] = out.astype(BF16)


def _matmul_res_ln(a, w, h, g, b):
    m, kdim = a.shape
    d = w.shape[1]
    tm = _pick(m, (256, 128))
    tk = _pick(kdim, (512, 256, 128))
    return pl.pallas_call(
        _mm_ln_kernel,
        out_shape=(jax.ShapeDtypeStruct((m, d), F32), jax.ShapeDtypeStruct((m, d), BF16)),
        grid=(m // tm, kdim // tk),
        in_specs=[pl.BlockSpec((tm, tk), lambda i, k: (i, k)),
                  pl.BlockSpec((tk, d), lambda i, k: (k, 0)),
                  pl.BlockSpec((tm, d), lambda i, k: (i, 0)),
                  pl.BlockSpec((1, d), lambda i, k: (0, 0)),
                  pl.BlockSpec((1, d), lambda i, k: (0, 0))],
        out_specs=(pl.BlockSpec((tm, d), lambda i, k: (i, 0)),
                   pl.BlockSpec((tm, d), lambda i, k: (i, 0))),
        scratch_shapes=[pltpu.VMEM((tm, d), F32)],
        compiler_params=_cparams(("parallel", "arbitrary")),
        name="matmul_res_ln",
    )(a, w, h, g.reshape(1, d), b.reshape(1, d))


def _log_sigmoid(x):
    return jnp.minimum(x, 0.0) - jnp.log1p(jnp.exp(-jnp.abs(x)))


def _gate_kernel(h_ref, wl_ref, w2_ref, b_ref, o_ref):
    low = _dot(h_ref[...], wl_ref[...])
    pre = _dot(low.astype(BF16), w2_ref[...]) + b_ref[...]
    o_ref[...] = _log_sigmoid(pre) / GLA_TAU


def _gla_gate(h_bf, w_low, w_gate2, b_gate):
    m, d = h_bf.shape
    qk = w_gate2.shape[1]
    tm = _pick(m, (256, 128))
    wl = jnp.zeros((d, LANES), BF16).at[:, :GLA_RANK].set(w_low.astype(BF16))
    w2 = jnp.zeros((LANES, qk), BF16).at[:GLA_RANK].set(w_gate2.astype(BF16))
    return pl.pallas_call(
        _gate_kernel,
        out_shape=jax.ShapeDtypeStruct((m, qk), F32),
        grid=(m // tm,),
        in_specs=[pl.BlockSpec((tm, d), lambda i: (i, 0)),
                  pl.BlockSpec((d, LANES), lambda i: (0, 0)),
                  pl.BlockSpec((LANES, qk), lambda i: (0, 0)),
                  pl.BlockSpec((1, qk), lambda i: (0, 0))],
        out_specs=pl.BlockSpec((tm, qk), lambda i: (i, 0)),
        compiler_params=_cparams(("parallel",)),
        name="gla_gate",
    )(h_bf, wl, w2, b_gate.reshape(1, qk))


def _head_norm_rows(o, gain):
    mu = jnp.mean(o, axis=-1, keepdims=True)
    var = jnp.mean(jnp.square(o - mu), axis=-1, keepdims=True)
    return (o - mu) * lax.rsqrt(var + LN_EPS) * gain


def _gla_kernel(q_ref, k_ref, v_ref, r_ref, g_ref, gain_ref, o_ref, s_ref, *, n_chunks, n_pad, dk):
    s_ref[...] = jnp.zeros_like(s_ref)
    row = lax.broadcasted_iota(I32, (CHUNK, CHUNK), 0)
    col = lax.broadcasted_iota(I32, (CHUNK, CHUNK), 1)
    tril = jnp.where(row >= col, 1.0, 0.0).astype(F32)
    first_valid = jnp.where(lax.broadcasted_iota(I32, (CHUNK, 1), 0) >= n_pad, 1.0, 0.0)
    gain = gain_ref[...]
    scale = dk ** -0.5

    def chunk(n, carry):
        sl = pl.ds(pl.multiple_of(n * CHUNK, CHUNK), CHUNK)
        valid = jnp.where(n == 0, first_valid, jnp.ones_like(first_valid))
        q = q_ref[sl, :] * scale * valid
        k = k_ref[sl, :] * valid
        v = v_ref[sl, :] * valid
        g = g_ref[sl, :] * valid
        b = _dot_hi(tril, g)
        b_last = b[CHUNK - 1:CHUNK, :]
        q_in = q * jnp.exp(b)
        k_out = k * jnp.exp(-b)
        k_end = k * jnp.exp(b_last - b)
        scores = _dot_nt(q_in.astype(BF16), k_out.astype(BF16)) * tril
        vb = v.astype(BF16)
        o = _dot(scores.astype(BF16), vb) + _dot_nt(q_in.astype(BF16), s_ref[...].astype(BF16))
        s_ref[...] = s_ref[...] * jnp.exp(b_last) + _dot_tn(vb, k_end.astype(BF16))
        r = r_ref[sl, :]
        o_ref[sl, :] = (_head_norm_rows(o, gain) * (r * jax.nn.sigmoid(r))).astype(o_ref.dtype)
        return carry

    lax.fori_loop(0, n_chunks, chunk, 0)


def _gla(proj3, gate3, norm_g, *, q_off, k_off, v_off, r_off, dk, dv):
    bsz, lp, _ = proj3.shape
    heads = GLA_HEADS
    kern = functools.partial(_gla_kernel, n_chunks=lp // CHUNK, n_pad=CHUNK - N_META, dk=dk)
    return pl.pallas_call(
        kern,
        out_shape=jax.ShapeDtypeStruct((bsz, lp, heads * dv), BF16),
        grid=(bsz, heads),
        in_specs=[pl.BlockSpec((None, lp, dk), lambda b, h: (b, 0, q_off // dk + h)),
                  pl.BlockSpec((None, lp, dk), lambda b, h: (b, 0, k_off // dk + h)),
                  pl.BlockSpec((None, lp, dv), lambda b, h: (b, 0, v_off // dv + h)),
                  pl.BlockSpec((None, lp, dv), lambda b, h: (b, 0, r_off // dv + h)),
                  pl.BlockSpec((None, lp, dk), lambda b, h: (b, 0, h)),
                  pl.BlockSpec((1, dv), lambda b, h: (0, h))],
        out_specs=pl.BlockSpec((None, lp, dv), lambda b, h: (b, 0, h)),
        scratch_shapes=[pltpu.VMEM((dv, dk), F32)],
        compiler_params=_cparams(("parallel", "parallel")),
        name="gla",
    )(proj3, proj3, proj3, proj3, gate3, norm_g.reshape(1, heads * dv))


def _ret_kernel(q_ref, k_ref, v_ref, gt_ref, cos_ref, sin_ref, dm_ref, xi_ref, zeta_ref, cd_ref, gain_ref,
                o_ref, s_ref, *, n_chunks, n_pad, dk):
    s_ref[...] = jnp.zeros_like(s_ref)
    first_valid = jnp.where(lax.broadcasted_iota(I32, (CHUNK, 1), 0) >= n_pad, 1.0, 0.0)
    gain = gain_ref[...]
    dmask = dm_ref[...]
    xi = xi_ref[...]
    zeta = zeta_ref[...]
    cd = cd_ref[...]
    scale = dk ** -0.5
    half = dk // 2

    def rot(t, cos, sin):
        t1, t2 = t[:, :half], t[:, half:]
        return jnp.concatenate([t1 * cos - t2 * sin, t1 * sin + t2 * cos], axis=-1)

    def chunk(n, carry):
        sl = pl.ds(pl.multiple_of(n * CHUNK, CHUNK), CHUNK)
        valid = jnp.where(n == 0, first_valid, jnp.ones_like(first_valid))
        cos, sin = cos_ref[sl, :], sin_ref[sl, :]
        q = rot(q_ref[sl, :], cos, sin) * valid
        k = rot(k_ref[sl, :], cos, sin) * (scale * valid)
        vb = (v_ref[sl, :] * valid).astype(BF16)
        scores = _dot_nt(q.astype(BF16), k.astype(BF16)) * dmask
        o = _dot(scores.astype(BF16), vb) + _dot((q * xi).astype(BF16), s_ref[...].astype(BF16))
        s_ref[...] = s_ref[...] * cd + _dot_tn((k * zeta).astype(BF16), vb)
        gt = gt_ref[sl, :]
        o_ref[sl, :] = (_head_norm_rows(o, gain) * (gt * jax.nn.sigmoid(gt))).astype(o_ref.dtype)
        return carry

    lax.fori_loop(0, n_chunks, chunk, 0)


def _retention(proj3, norm_g, *, dk, dv):
    bsz, lp, _ = proj3.shape
    heads = RET_HEADS
    half = dk // 2
    pos_tok = jnp.arange(lp, dtype=F32) - float(CHUNK - N_META)
    inv = ROPE_BASE ** (-jnp.arange(0, dk, 2, dtype=F32) / dk)
    ang = pos_tok[:, None] * inv[None, :]
    cos_t, sin_t = jnp.cos(ang), jnp.sin(ang)
    log_gamma = jnp.log(1.0 - 2.0 ** (-5.0 - jnp.arange(heads, dtype=F32)))
    pos = jnp.arange(CHUNK, dtype=F32)
    diff = pos[:, None] - pos[None, :]
    dmask = jnp.where(diff >= 0, jnp.exp(log_gamma[:, None, None] * jnp.maximum(diff, 0.0)), 0.0)
    xi = jnp.exp(log_gamma[:, None] * (pos + 1.0))[:, :, None]
    zeta = jnp.exp(log_gamma[:, None] * (CHUNK - 1.0 - pos))[:, :, None]
    cdec = jnp.exp(log_gamma * CHUNK)[:, None, None]
    kern = functools.partial(_ret_kernel, n_chunks=lp // CHUNK, n_pad=CHUNK - N_META, dk=dk)
    nq = heads
    nv = (2 * heads * dk) // dv
    return pl.pallas_call(
        kern,
        out_shape=jax.ShapeDtypeStruct((bsz, lp, heads * dv), BF16),
        grid=(bsz, heads),
        in_specs=[pl.BlockSpec((None, lp, dk), lambda b, h: (b, 0, h)),
                  pl.BlockSpec((None, lp, dk), lambda b, h: (b, 0, nq + h)),
                  pl.BlockSpec((None, lp, dv), lambda b, h: (b, 0, nv + h)),
                  pl.BlockSpec((None, lp, dv), lambda b, h: (b, 0, nv + heads + h)),
                  pl.BlockSpec((lp, half), lambda b, h: (0, 0)),
                  pl.BlockSpec((lp, half), lambda b, h: (0, 0)),
                  pl.BlockSpec((None, CHUNK, CHUNK), lambda b, h: (h, 0, 0)),
                  pl.BlockSpec((None, CHUNK, 1), lambda b, h: (h, 0, 0)),
                  pl.BlockSpec((None, CHUNK, 1), lambda b, h: (h, 0, 0)),
                  pl.BlockSpec((None, 1, 1), lambda b, h: (h, 0, 0)),
                  pl.BlockSpec((1, dv), lambda b, h: (0, h))],
        out_specs=pl.BlockSpec((None, lp, dv), lambda b, h: (b, 0, h)),
        scratch_shapes=[pltpu.VMEM((dk, dv), F32)],
        compiler_params=_cparams(("parallel", "parallel")),
        name="retention",
    )(proj3, proj3, proj3, proj3, cos_t, sin_t, dmask, xi, zeta, cdec, norm_g.reshape(1, heads * dv))


def _s5_prepare(log_dt, lam_re, lam_im, b_re, b_im, c_re, c_im, n_scan):
    cs = S5_CHUNK
    groups, states = lam_re.shape
    dt = jnp.exp(log_dt.astype(F32))[:, None]
    lr, li = lam_re.astype(F32), lam_im.astype(F32)
    mag = jnp.exp(lr * dt)
    ab_re, ab_im = mag * jnp.cos(li * dt), mag * jnp.sin(li * dt)
    den = lr * lr + li * li
    nr, ni = ab_re - 1.0, ab_im
    z_re = (nr * lr + ni * li) / den
    z_im = (ni * lr - nr * li) / den
    bz_re = z_re[..., None] * b_re - z_im[..., None] * b_im
    bz_im = z_re[..., None] * b_im + z_im[..., None] * b_re

    def power(kk):
        kk = kk.astype(F32)[..., None, None]
        m = jnp.exp(kk * (lr * dt))
        return m * jnp.cos(kk * (li * dt)), m * jnp.sin(kk * (li * dt))

    pw_re, pw_im = power(jnp.arange(cs + 1))
    cp_re = c_re[None] * pw_re[:, :, None, :] - c_im[None] * pw_im[:, :, None, :]
    cp_im = c_re[None] * pw_im[:, :, None, :] + c_im[None] * pw_re[:, :, None, :]
    hi = lax.Precision.HIGHEST
    kern = (jnp.einsum('tgcp,gpd->tgcd', cp_re, bz_re, precision=hi)
            - jnp.einsum('tgcp,gpd->tgcd', cp_im, bz_im, precision=hi))
    s_idx = jnp.arange(cs)[:, None]
    t_idx = jnp.arange(cs)[None, :]
    lag = jnp.clip(t_idx - s_idx, 0, cs - 1)
    toep = jnp.where((t_idx >= s_idx)[:, :, None, None, None], kern[lag], 0.0)
    toep = toep.transpose(2, 0, 4, 1, 3).reshape(groups, cs * S5_GROUP, cs * S5_GROUP)
    rev = pw_re[cs - 1 - jnp.arange(cs)], pw_im[cs - 1 - jnp.arange(cs)]
    win_re = rev[0][..., None] * bz_re[None] - rev[1][..., None] * bz_im[None]
    win_im = rev[0][..., None] * bz_im[None] + rev[1][..., None] * bz_re[None]
    win = jnp.concatenate([win_re, win_im], axis=2)
    win = win.transpose(1, 0, 3, 2).reshape(groups, cs * S5_GROUP, 2 * states)
    wout = jnp.concatenate([cp_re[1:], -cp_im[1:]], axis=3)
    wout = wout.transpose(1, 3, 0, 2).reshape(groups, 2 * states, cs * S5_GROUP)
    sc_re, sc_im = power(cs * (2 ** jnp.arange(n_scan)))
    mul_r = jnp.concatenate([sc_re, sc_re], axis=-1)
    mul_i = jnp.concatenate([-sc_im, sc_im], axis=-1)
    scan_mul = jnp.stack([mul_r, mul_i], axis=2).transpose(1, 0, 2, 3)
    return toep.astype(BF16), win.astype(BF16), wout.astype(BF16), scan_mul


def _s5_kernel(x_ref, toep_ref, win_ref, wout_ref, mul_ref, y_ref, *, bsz, n_steps, n_scan, n_pad_steps):
    rows = bsz * n_steps
    states2 = win_ref.shape[-1]
    step = lax.broadcasted_iota(I32, (rows, 1), 0)
    for bb in range(1, bsz):
        step = step - jnp.where(lax.broadcasted_iota(I32, (rows, 1), 0) >= bb * n_steps, n_steps, 0)
    x = jnp.where(step >= n_pad_steps, x_ref[...], 0.0).astype(BF16)
    y = _dot(x, toep_ref[...])
    acc = _dot(x, win_ref[...])
    for j in range(n_scan):
        sh = 2 ** j
        prev = jnp.where(step >= sh, pltpu.roll(acc, sh, 0), 0.0)
        swapped = pltpu.roll(prev, states2 // 2, 1)
        acc = acc + mul_ref[j, 0:1, :] * prev + mul_ref[j, 1:2, :] * swapped
    start = jnp.where(step >= 1, pltpu.roll(acc, 1, 0), 0.0)
    y_ref[...] = y + _dot(start.astype(BF16), wout_ref[...])


def _s5_scan(x, toep, win, wout, scan_mul, *, bsz, n_steps, n_pad_steps):
    groups, rows, width = x.shape
    states2 = win.shape[-1]
    n_scan = scan_mul.shape[1]
    kern = functools.partial(_s5_kernel, bsz=bsz, n_steps=n_steps, n_scan=n_scan, n_pad_steps=n_pad_steps)
    return pl.pallas_call(
        kern,
        out_shape=jax.ShapeDtypeStruct((groups, rows, width), F32),
        grid=(groups,),
        in_specs=[pl.BlockSpec((None, rows, width), lambda g: (g, 0, 0)),
                  pl.BlockSpec((None, width, width), lambda g: (g, 0, 0)),
                  pl.BlockSpec((None, width, states2), lambda g: (g, 0, 0)),
                  pl.BlockSpec((None, states2, width), lambda g: (g, 0, 0)),
                  pl.BlockSpec((None, n_scan, 2, states2), lambda g: (g, 0, 0, 0))],
        out_specs=pl.BlockSpec((None, rows, width), lambda g: (g, 0, 0)),
        compiler_params=_cparams(("parallel",)),
        name="s5_scan",
    )(x, toep, win, wout, scan_mul)


def _glu_kernel(y_ref, u_ref, d_ref, w_ref, o_ref):
    z = jax.nn.gelu(y_ref[...] + d_ref[...] * u_ref[...])
    o_ref[...] = (z * jax.nn.sigmoid(_dot(z.astype(BF16), w_ref[...]))).astype(o_ref.dtype)


def _s5_glu(y, proj, d, w_glu_bf):
    m, width = y.shape
    tm = _pick(m, (256, 128))
    return pl.pallas_call(
        _glu_kernel,
        out_shape=jax.ShapeDtypeStruct((m, width), BF16),
        grid=(m // tm,),
        in_specs=[pl.BlockSpec((tm, width), lambda i: (i, 0)),
                  pl.BlockSpec((tm, width), lambda i: (i, 0)),
                  pl.BlockSpec((1, width), lambda i: (0, 0)),
                  pl.BlockSpec((width, width), lambda i: (0, 0))],
        out_specs=pl.BlockSpec((tm, width), lambda i: (i, 0)),
        compiler_params=_cparams(("parallel",)),
        name="s5_glu",
    )(y, proj, d.reshape(1, width), w_glu_bf)


def _s5_mixer(proj, bsz, lp, params):
    log_dt, lam_re, lam_im, b_re, b_im, c_re, c_im, d, w_glu = params
    groups = lam_re.shape[0]
    width = groups * S5_GROUP
    cs = S5_CHUNK
    n_real = lp // cs
    n_steps = -(-n_real // 8) * 8
    n_scan = max(1, math.ceil(math.log2(n_steps)))
    toep, win, wout, scan_mul = _s5_prepare(log_dt, lam_re, lam_im, b_re, b_im, c_re, c_im, n_scan)
    u = proj[:, :width].reshape(bsz, n_real, cs, groups, S5_GROUP)
    x = u.transpose(3, 0, 1, 2, 4).reshape(groups, bsz, n_real, cs * S5_GROUP)
    x = jnp.pad(x, ((0, 0), (0, 0), (0, n_steps - n_real), (0, 0))).reshape(groups, bsz * n_steps, cs * S5_GROUP)
    y = _s5_scan(x, toep, win, wout, scan_mul, bsz=bsz, n_steps=n_steps,
                 n_pad_steps=(CHUNK - N_META) // cs)
    y = y.reshape(groups, bsz, n_steps, cs, S5_GROUP)[:, :, :n_real]
    y = y.transpose(1, 2, 3, 0, 4).reshape(bsz * lp, width)
    return _s5_glu(y, proj, d, w_glu.astype(BF16))


def _router_kernel(x_ref, rt_ref, bias_ref, idx_ref, pos_ref, w_ref, cnt_ref, run_ref, *, tm):
    i = pl.program_id(0)

    @pl.when(i == 0)
    def _():
        run_ref[...] = jnp.zeros_like(run_ref)

    ne, ng = N_EXPERTS, N_GROUPS
    per = ne // ng
    neg = -jnp.inf
    logits = lax.dot_general(rt_ref[...], x_ref[...], (((1,), (1,)), ((), ())),
                             preferred_element_type=F32, precision=lax.Precision.HIGHEST)
    s = jax.nn.sigmoid(logits)
    sb = s + bias_ref[...]
    sb3 = sb.reshape(ng, per, tm)
    io = lax.broadcasted_iota(I32, (ng, per, tm), 1)
    m1 = jnp.max(sb3, axis=1, keepdims=True)
    first = jnp.min(jnp.where(sb3 == m1, io, per), axis=1, keepdims=True)
    m2 = jnp.max(jnp.where(io == first, neg, sb3), axis=1, keepdims=True)
    gs = m1 + m2
    gi = lax.broadcasted_iota(I32, (ng, 1, tm), 0)
    grank = jnp.zeros((ng, 1, tm), F32)
    for g2 in range(ng):
        other = gs[g2:g2 + 1]
        beats = (other > gs) | ((other == gs) & (g2 < gi))
        grank = grank + jnp.where(beats, 1.0, 0.0)
    gsel = grank < float(TOPK_GROUPS)
    ms = jnp.where(gsel, sb3, neg).reshape(ne, tm)
    ei = lax.broadcasted_iota(I32, (ne, tm), 0)
    rank = jnp.zeros((ne, tm), F32)
    for e2 in range(ne):
        other = ms[e2:e2 + 1, :]
        beats = (other > ms) | ((other == ms) & (e2 < ei))
        rank = rank + jnp.where(beats, 1.0, 0.0)
    sel = rank < float(TOP_K)
    self_ = jnp.where(sel, 1.0, 0.0)
    wsum = jnp.sum(jnp.where(sel, s, 0.0), axis=0, keepdims=True)
    wgt = s / wsum * ROUTE_SCALE
    ti = lax.broadcasted_iota(I32, (tm, tm), 0)
    tj = lax.broadcasted_iota(I32, (tm, tm), 1)
    upper = jnp.where(ti < tj, 1.0, 0.0).astype(BF16)
    pos = _dot(self_.astype(BF16), upper) + run_ref[:, 0:1]
    run_ref[...] = run_ref[...] + jnp.sum(self_, axis=1, keepdims=True)
    li = lax.broadcasted_iota(I32, (ne, ne), 0)
    lj = lax.broadcasted_iota(I32, (ne, ne), 1)
    lower = jnp.where(lj < li, 1.0, 0.0).astype(BF16)
    slot = _dot(lower, self_.astype(BF16))
    eif = ei.astype(F32)
    idx_rows, pos_rows, w_rows = [], [], []
    for kk in range(TOP_K):
        one = sel & (slot == float(kk))
        idx_rows.append(jnp.sum(jnp.where(one, eif, 0.0), axis=0, keepdims=True))
        pos_rows.append(jnp.sum(jnp.where(one, pos, 0.0), axis=0, keepdims=True))
        w_rows.append(jnp.sum(jnp.where(one, wgt, 0.0), axis=0, keepdims=True))
    idx_ref[...] = jnp.concatenate(idx_rows, axis=0).astype(I32)
    pos_ref[...] = jnp.concatenate(pos_rows, axis=0).astype(I32)
    w_ref[...] = jnp.concatenate(w_rows, axis=0)

    @pl.when(i == pl.num_programs(0) - 1)
    def _():
        cnt_ref[...] = run_ref[...].astype(I32)


def _router(h, router, bias):
    m, d = h.shape
    ne = router.shape[1]
    tm = _pick(m, (256, 128))
    kern = functools.partial(_router_kernel, tm=tm)
    return pl.pallas_call(
        kern,
        out_shape=(jax.ShapeDtypeStruct((TOP_K, m), I32), jax.ShapeDtypeStruct((TOP_K, m), I32),
                   jax.ShapeDtypeStruct((TOP_K, m), F32), jax.ShapeDtypeStruct((ne, LANES), I32)),
        grid=(m // tm,),
        in_specs=[pl.BlockSpec((tm, d), lambda i: (i, 0)),
                  pl.BlockSpec((ne, d), lambda i: (0, 0)),
                  pl.BlockSpec((ne, 1), lambda i: (0, 0))],
        out_specs=(pl.BlockSpec((TOP_K, tm), lambda i: (0, i)),
                   pl.BlockSpec((TOP_K, tm), lambda i: (0, i)),
                   pl.BlockSpec((TOP_K, tm), lambda i: (0, i)),
                   pl.BlockSpec((ne, LANES), lambda i: (0, 0))),
        scratch_shapes=[pltpu.VMEM((ne, LANES), F32)],
        compiler_params=_cparams(("arbitrary",)),
        name="moe_router",
    )(h, router.T, bias.reshape(ne, 1))


def _dispatch_kernel(dest_ref, x_ref, init_ref, xs_ref, sem, *, tm):
    del init_ref

    def copy(r, kk):
        return pltpu.make_async_copy(x_ref.at[pl.ds(r, 1)], xs_ref.at[pl.ds(dest_ref[kk, r], 1)], sem)

    def issue(r, c):
        for kk in range(TOP_K):
            copy(r, kk).start()
        return c

    def drain(r, c):
        for kk in range(TOP_K):
            copy(r, kk).wait()
        return c

    lax.fori_loop(0, tm, issue, 0)
    lax.fori_loop(0, tm, drain, 0)


def _dispatch(x, dest, n_rows):
    m, d = x.shape
    tm = _pick(m, (256, 128))
    kern = functools.partial(_dispatch_kernel, tm=tm)
    return pl.pallas_call(
        kern,
        out_shape=jax.ShapeDtypeStruct((n_rows, d), x.dtype),
        grid=(m // tm,),
        in_specs=[pl.BlockSpec((TOP_K, tm), lambda i: (0, i), memory_space=pltpu.SMEM),
                  pl.BlockSpec((tm, d), lambda i: (i, 0)),
                  pl.BlockSpec(memory_space=pl.ANY)],
        out_specs=pl.BlockSpec(memory_space=pl.ANY),
        scratch_shapes=[pltpu.SemaphoreType.DMA],
        input_output_aliases={2: 0},
        compiler_params=_cparams(("arbitrary",)),
        name="moe_dispatch",
    )(dest, x, jnp.zeros((n_rows, d), x.dtype))


def _ffn_kernel(be_ref, nb_ref, x_ref, wg_ref, wu_ref, wd_ref, y_ref, wgb_ref, wub_ref, wdb_ref):
    i = pl.program_id(0)
    prev = be_ref[jnp.maximum(i - 1, 0)]
    fresh = (i == 0) | (be_ref[i] != prev)

    @pl.when(fresh & (i < nb_ref[0]))
    def _():
        wgb_ref[...] = wg_ref[...].astype(BF16)
        wub_ref[...] = wu_ref[...].astype(BF16)
        wdb_ref[...] = wd_ref[...].astype(BF16)

    @pl.when(i < nb_ref[0])
    def _():
        x = x_ref[...].astype(BF16)
        a = _dot(x, wgb_ref[...])
        u = _dot(x, wub_ref[...])
        hmid = (a * jax.nn.sigmoid(a)) * u
        y_ref[...] = _dot(hmid.astype(BF16), wdb_ref[...]).astype(y_ref.dtype)

    @pl.when(i >= nb_ref[0])
    def _():
        y_ref[...] = jnp.zeros_like(y_ref)


def _ffn(xs, block_e, n_active, layer, w_gate, w_up, w_down, tm, out_dtype=F32):
    n_rows, d = xs.shape
    ff = w_gate.shape[-1]
    nb = n_rows // tm

    def xmap(i, be, nbr):
        return (jnp.minimum(i, nbr[0] - 1), 0)

    def wmap(i, be, nbr):
        return (layer, be[i], 0, 0)

    return pl.pallas_call(
        _ffn_kernel,
        out_shape=jax.ShapeDtypeStruct((n_rows, d), out_dtype),
        grid_spec=pltpu.PrefetchScalarGridSpec(
            num_scalar_prefetch=2,
            grid=(nb,),
            in_specs=[pl.BlockSpec((tm, d), xmap),
                      pl.BlockSpec((None, None, d, ff), wmap),
                      pl.BlockSpec((None, None, d, ff), wmap),
                      pl.BlockSpec((None, None, ff, d), wmap)],
            out_specs=pl.BlockSpec((tm, d), lambda i, be, nbr: (i, 0)),
            scratch_shapes=[pltpu.VMEM((d, ff), BF16), pltpu.VMEM((d, ff), BF16), pltpu.VMEM((ff, d), BF16)]),
        compiler_params=_cparams(("arbitrary",)),
        name="moe_ffn",
    )(block_e, n_active, xs, w_gate, w_up, w_down)


def _combine_kernel(dest_ref, ys_ref, w_ref, sh_ref, h_ref, g_ref, b_ref, o_ref, ob_ref, buf_ref, sem, *, tm):
    def copy(r, kk):
        return pltpu.make_async_copy(ys_ref.at[pl.ds(dest_ref[kk, r], 1)],
                                     buf_ref.at[kk, pl.ds(r, 1)], sem)

    def issue(r, c):
        for kk in range(TOP_K):
            copy(r, kk).start()
        return c

    def drain(r, c):
        for kk in range(TOP_K):
            copy(r, kk).wait()
        return c

    lax.fori_loop(0, tm, issue, 0)
    lax.fori_loop(0, tm, drain, 0)
    w = w_ref[...]
    routed = w[:, 0:1] * buf_ref[0]
    for kk in range(1, TOP_K):
        routed = routed + w[:, kk:kk + 1] * buf_ref[kk]
    y = DEEPNORM_ALPHA * h_ref[...] + (routed + sh_ref[...])
    out = _layer_norm_rows(y, g_ref[...], b_ref[...])
    o_ref[...] = out
    ob_ref[...] = out.astype(BF16)


def _combine_ln(ys, dest, w_t, shared, h, g, b):
    m, d = h.shape
    tm = _pick(m, (128,))
    kern = functools.partial(_combine_kernel, tm=tm)
    return pl.pallas_call(
        kern,
        out_shape=(jax.ShapeDtypeStruct((m, d), F32), jax.ShapeDtypeStruct((m, d), BF16)),
        grid=(m // tm,),
        in_specs=[pl.BlockSpec((TOP_K, tm), lambda i: (0, i), memory_space=pltpu.SMEM),
                  pl.BlockSpec(memory_space=pl.ANY),
                  pl.BlockSpec((tm, TOP_K), lambda i: (i, 0)),
                  pl.BlockSpec((tm, d), lambda i: (i, 0)),
                  pl.BlockSpec((tm, d), lambda i: (i, 0)),
                  pl.BlockSpec((1, d), lambda i: (0, 0)),
                  pl.BlockSpec((1, d), lambda i: (0, 0))],
        out_specs=(pl.BlockSpec((tm, d), lambda i: (i, 0)),
                   pl.BlockSpec((tm, d), lambda i: (i, 0))),
        scratch_shapes=[pltpu.VMEM((TOP_K, tm, d), F32), pltpu.SemaphoreType.DMA],
        compiler_params=_cparams(("arbitrary",)),
        name="moe_combine_ln",
    )(dest, ys, w_t, shared, h, g.reshape(1, d), b.reshape(1, d))


def _moe_ln(h, h_bf, layer, router, bias, w_gate, w_up, w_down, s_gate, s_up, s_down, ln_g, ln_b):
    m, d = h.shape
    tm = _pick(m, (256, 128))
    idx, pos, wsel, counts = _router(h, router, bias)
    counts = counts[:, 0]
    padded = ((counts + tm - 1) // tm) * tm
    pend = jnp.cumsum(padded)
    pstart = pend - padded
    experts = jnp.arange(N_EXPERTS, dtype=I32)
    dest = jnp.sum(jnp.where(idx[:, :, None] == experts, pstart, 0), axis=-1) + pos
    nb = (m * TOP_K) // tm + N_EXPERTS
    block_row = jnp.arange(nb, dtype=I32)[:, None] * tm
    block_e = jnp.minimum(jnp.sum((pend[None, :] <= block_row).astype(I32), axis=1), N_EXPERTS - 1)
    n_active = (pend[-1:] // tm).astype(I32)
    xs = _dispatch(h, dest, nb * tm)
    ys = _ffn(xs, block_e, n_active, layer, w_gate, w_up, w_down, tm)
    shared = _ffn(h_bf, jnp.zeros((m // tm,), I32), jnp.full((1,), m // tm, I32), layer,
                  s_gate[:, None], s_up[:, None], s_down[:, None], tm)
    return _combine_ln(ys, dest, wsel.T, shared, h, ln_g, ln_b)


def kernel(x, meta, ab_w_in, s5_log_dt, s5_lambda_re, s5_lambda_im, s5_b_re, s5_b_im, s5_c_re, s5_c_im, s5_d, s5_w_glu, gla_w_gate2, gla_b_gate, gla_norm_g, ab_w_out, ret_w_in, ret_norm_g, ret_w_out, ln1_g, ln1_b, ln2_g, ln2_b, moe_router, moe_bias, moe_w_gate, moe_w_up, moe_w_down, shared_w_gate, shared_w_up, shared_w_down):
    bsz, seq, d = x.shape
    length = seq + N_META
    pad = (-length) % CHUNK
    assert pad == CHUNK - N_META, "sequence length must be a multiple of the mixer chunk"
    lp = length + pad
    m = bsz * lp
    h3 = jnp.concatenate([jnp.zeros((bsz, pad, d), x.dtype),
                          jnp.broadcast_to(meta[None].astype(x.dtype), (bsz, N_META, d)), x], axis=1)
    h = h3.reshape(m, d)
    h_bf = h.astype(BF16)
    depth = ln1_g.shape[0]
    for layer in range(depth):
        i = layer // 2
        if layer % 2 == 0:
            s5_w = s5_lambda_re.shape[1] * S5_GROUP
            gla_qk = gla_w_gate2.shape[2]
            gla_w = gla_norm_g.shape[1]
            n_main = s5_w + 2 * gla_qk + 2 * gla_w
            w_in = ab_w_in[i].astype(BF16)
            proj = _matmul(h_bf, w_in, n_main)
            gate = _gla_gate(h_bf, ab_w_in[i][:, n_main:], gla_w_gate2[i], gla_b_gate[i])
            y_a = _s5_mixer(proj, bsz, lp, (s5_log_dt[i], s5_lambda_re[i], s5_lambda_im[i], s5_b_re[i], s5_b_im[i],
                                            s5_c_re[i], s5_c_im[i], s5_d[i], s5_w_glu[i]))
            o = _gla(proj.reshape(bsz, lp, n_main), gate.reshape(bsz, lp, gla_qk), gla_norm_g[i],
                     q_off=s5_w, k_off=s5_w + gla_qk, v_off=s5_w + 2 * gla_qk, r_off=s5_w + 2 * gla_qk + gla_w,
                     dk=gla_qk // GLA_HEADS, dv=gla_w // GLA_HEADS)
            mixed = jnp.concatenate([y_a, o.reshape(m, gla_w)], axis=1)
            w_out = ab_w_out[i].astype(BF16)
        else:
            ret_w = ret_norm_g.shape[1]
            ret_qk = (ret_w_in.shape[2] - 2 * ret_w) // 2
            proj = _matmul(h_bf, ret_w_in[i].astype(BF16), ret_w_in.shape[2])
            o = _retention(proj.reshape(bsz, lp, ret_w_in.shape[2]), ret_norm_g[i],
                           dk=ret_qk // RET_HEADS, dv=ret_w // RET_HEADS)
            mixed = o.reshape(m, ret_w)
            w_out = ret_w_out[i].astype(BF16)
        h, h_bf = _matmul_res_ln(mixed, w_out, h, ln1_g[layer], ln1_b[layer])
        h, h_bf = _moe_ln(h, h_bf, layer, moe_router[layer], moe_bias[layer], moe_w_gate, moe_w_up, moe_w_down,
                          shared_w_gate, shared_w_up, shared_w_down, ln2_g[layer], ln2_b[layer])
    return h.reshape(bsz, lp, d)[:, pad + N_META:]
```

```python
import functools
import math

import jax
import jax.numpy as jnp
import numpy as np
from jax import lax
from jax.experimental import pallas as pl
from jax.experimental.pallas import tpu as pltpu

F32 = jnp.float32
BF16 = jnp.bfloat16
I32 = jnp.int32

N_META = 16
CHUNK = 64
LN_EPS = 1e-5
S5_GROUP = 16
S5_STATE = 64
S5_CHUNK = 16
GLA_HEADS = 4
GLA_RANK = 16
GLA_TAU = 16.0
RET_HEADS = 8
ROPE_BASE = 10000.0
N_EXPERTS = 64
TOP_K = 8
N_GROUPS = 8
TOPK_GROUPS = 4
ROUTE_SCALE = 2.5
DEPTH = 2
DEEPNORM_ALPHA = (2 * DEPTH) ** 0.25

LANES = 128
VMEM_LIMIT = 56 * 1024 * 1024


def _cparams(sem):
    return pltpu.CompilerParams(dimension_semantics=sem, vmem_limit_bytes=VMEM_LIMIT)


def _pick(n, cands):
    for c in cands:
        if n % c == 0:
            return c
    raise ValueError(f"no tile for {n} in {cands}")


def _dot(a, b):
    return jnp.dot(a, b, preferred_element_type=F32)


def _dot_nt(a, b):
    return lax.dot_general(a, b, (((1,), (1,)), ((), ())), preferred_element_type=F32)


def _dot_tn(a, b):
    return lax.dot_general(a, b, (((0,), (0,)), ((), ())), preferred_element_type=F32)


def _dot_hi(a, b):
    return jnp.dot(a, b, preferred_element_type=F32, precision=lax.Precision.HIGHEST)


def _mm_kernel(a_ref, w_ref, o_ref):
    o_ref[...] = _dot(a_ref[...], w_ref[...]).astype(o_ref.dtype)


def _matmul(a, w, col0, n_cols, out_dtype):
    m, kdim = a.shape
    tm = _pick(m, (1408, 768, 384, 128))
    tn = _pick(math.gcd(n_cols, col0) if col0 else n_cols, (512, 256, 128))
    j0 = col0 // tn
    return pl.pallas_call(
        _mm_kernel,
        out_shape=jax.ShapeDtypeStruct((m, n_cols), out_dtype),
        grid=(m // tm, n_cols // tn),
        in_specs=[pl.BlockSpec((tm, kdim), lambda i, j: (i, 0)),
                  pl.BlockSpec((kdim, tn), lambda i, j: (0, j0 + j))],
        out_specs=pl.BlockSpec((tm, tn), lambda i, j: (i, j)),
        compiler_params=_cparams(("parallel", "arbitrary")),
        name="matmul",
    )(a, w)


def _layer_norm_rows(y, g, b):
    mu = jnp.mean(y, axis=-1, keepdims=True)
    var = jnp.mean(jnp.square(y - mu), axis=-1, keepdims=True)
    return (y - mu) * lax.rsqrt(var + LN_EPS) * g + b


def _mm_ln_kernel(a_ref, w_ref, h_ref, g_ref, b_ref, o_ref, ob_ref, acc_ref):
    k = pl.program_id(1)

    @pl.when(k == 0)
    def _():
        acc_ref[...] = jnp.zeros_like(acc_ref)

    acc_ref[...] += _dot(a_ref[...], w_ref[...])

    @pl.when(k == pl.num_programs(1) - 1)
    def _():
        y = DEEPNORM_ALPHA * h_ref[...] + acc_ref[...]
        out = _layer_norm_rows(y, g_ref[...], b_ref[...])
        o_ref[...] = out
        ob_ref[...] = out.astype(BF16)


def _matmul_res_ln(a, w, h, g, b):
    m, kdim = a.shape
    d = w.shape[1]
    tm = _pick(m, (384, 128))
    tk = _pick(kdim, (512, 256, 128))
    return pl.pallas_call(
        _mm_ln_kernel,
        out_shape=(jax.ShapeDtypeStruct((m, d), F32), jax.ShapeDtypeStruct((m, d), BF16)),
        grid=(m // tm, kdim // tk),
        in_specs=[pl.BlockSpec((tm, tk), lambda i, k: (i, k)),
                  pl.BlockSpec((tk, d), lambda i, k: (k, 0)),
                  pl.BlockSpec((tm, d), lambda i, k: (i, 0)),
                  pl.BlockSpec((1, d), lambda i, k: (0, 0)),
                  pl.BlockSpec((1, d), lambda i, k: (0, 0))],
        out_specs=(pl.BlockSpec((tm, d), lambda i, k: (i, 0)),
                   pl.BlockSpec((tm, d), lambda i, k: (i, 0))),
        scratch_shapes=[pltpu.VMEM((tm, d), F32)],
        compiler_params=_cparams(("parallel", "arbitrary")),
        name="matmul_res_ln",
    )(a, w, h, g.reshape(1, d), b.reshape(1, d))


def _log_sigmoid(x):
    return jnp.minimum(x, 0.0) - jnp.log1p(jnp.exp(-jnp.abs(x)))


def _gate_kernel(h_ref, wl_ref, w2_ref, b_ref, o_ref):
    low = _dot(h_ref[...], wl_ref[...])
    pre = _dot(low.astype(BF16), w2_ref[...]) + b_ref[...]
    o_ref[...] = _log_sigmoid(pre) / GLA_TAU


def _gla_gate(h_bf, w_low, w_gate2, b_gate):
    m, d = h_bf.shape
    qk = w_gate2.shape[1]
    tm = _pick(m, (256, 128))
    wl = jnp.zeros((d, LANES), BF16).at[:, :GLA_RANK].set(w_low.astype(BF16))
    w2 = jnp.zeros((LANES, qk), BF16).at[:GLA_RANK].set(w_gate2.astype(BF16))
    return pl.pallas_call(
        _gate_kernel,
        out_shape=jax.ShapeDtypeStruct((m, qk), F32),
        grid=(m // tm,),
        in_specs=[pl.BlockSpec((tm, d), lambda i: (i, 0)),
                  pl.BlockSpec((d, LANES), lambda i: (0, 0)),
                  pl.BlockSpec((LANES, qk), lambda i: (0, 0)),
                  pl.BlockSpec((1, qk), lambda i: (0, 0))],
        out_specs=pl.BlockSpec((tm, qk), lambda i: (i, 0)),
        compiler_params=_cparams(("parallel",)),
        name="gla_gate",
    )(h_bf, wl, w2, b_gate.reshape(1, qk))


def _head_norm_rows(o, gain):
    mu = jnp.mean(o, axis=-1, keepdims=True)
    var = jnp.mean(jnp.square(o - mu), axis=-1, keepdims=True)
    return (o - mu) * lax.rsqrt(var + LN_EPS) * gain


HEADS_PER_STEP = 2


def _gla_kernel(q_ref, k_ref, v_ref, r_ref, g_ref, gain_ref, o_ref, s_ref, *, n_chunks, n_pad, dk, dv):
    s_ref[...] = jnp.zeros_like(s_ref)
    row = lax.broadcasted_iota(I32, (CHUNK, CHUNK), 0)
    col = lax.broadcasted_iota(I32, (CHUNK, CHUNK), 1)
    tril = jnp.where(row >= col, 1.0, 0.0).astype(F32)
    first_valid = jnp.where(lax.broadcasted_iota(I32, (CHUNK, 1), 0) >= n_pad, 1.0, 0.0)
    scale = dk ** -0.5

    def chunk(n, carry):
        sl = pl.ds(pl.multiple_of(n * CHUNK, CHUNK), CHUNK)
        valid = jnp.where(n == 0, first_valid, jnp.ones_like(first_valid))
        for hh in range(HEADS_PER_STEP):
            ck = slice(hh * dk, (hh + 1) * dk)
            cv = slice(hh * dv, (hh + 1) * dv)
            q = q_ref[sl, ck].astype(F32) * scale * valid
            k = k_ref[sl, ck].astype(F32) * valid
            vb = (v_ref[sl, cv].astype(F32) * valid).astype(BF16)
            g = g_ref[sl, ck] * valid
            b = _dot_hi(tril, g)
            b_last = b[CHUNK - 1:CHUNK, :]
            q_in = (q * jnp.exp(b)).astype(BF16)
            k_out = k * jnp.exp(-b)
            k_end = k * jnp.exp(b_last - b)
            scores = _dot_nt(q_in, k_out.astype(BF16)) * tril
            o = _dot(scores.astype(BF16), vb) + _dot_nt(q_in, s_ref[hh].astype(BF16))
            s_ref[hh] = s_ref[hh] * jnp.exp(b_last) + _dot_tn(vb, k_end.astype(BF16))
            r = r_ref[sl, cv].astype(F32)
            o_ref[sl, cv] = (_head_norm_rows(o, gain_ref[:, cv]) * (r * jax.nn.sigmoid(r))).astype(o_ref.dtype)
        return carry

    lax.fori_loop(0, n_chunks, chunk, 0)


def _gla(proj3, gate3, norm_g, *, q_off, k_off, v_off, r_off, dk, dv):
    bsz, lp, _ = proj3.shape
    heads = GLA_HEADS
    hp = HEADS_PER_STEP
    wk, wv = hp * dk, hp * dv
    kern = functools.partial(_gla_kernel, n_chunks=lp // CHUNK, n_pad=CHUNK - N_META, dk=dk, dv=dv)
    return pl.pallas_call(
        kern,
        out_shape=jax.ShapeDtypeStruct((bsz, lp, heads * dv), BF16),
        grid=(bsz, heads // hp),
        in_specs=[pl.BlockSpec((None, lp, wk), lambda b, h: (b, 0, q_off // wk + h)),
                  pl.BlockSpec((None, lp, wk), lambda b, h: (b, 0, k_off // wk + h)),
                  pl.BlockSpec((None, lp, wv), lambda b, h: (b, 0, v_off // wv + h)),
                  pl.BlockSpec((None, lp, wv), lambda b, h: (b, 0, r_off // wv + h)),
                  pl.BlockSpec((None, lp, wk), lambda b, h: (b, 0, h)),
                  pl.BlockSpec((1, wv), lambda b, h: (0, h))],
        out_specs=pl.BlockSpec((None, lp, wv), lambda b, h: (b, 0, h)),
        scratch_shapes=[pltpu.VMEM((hp, dv, dk), F32)],
        compiler_params=_cparams(("parallel", "parallel")),
        name="gla",
    )(proj3, proj3, proj3, proj3, gate3, norm_g.reshape(1, heads * dv))


def _ret_kernel(q_ref, k_ref, v_ref, gt_ref, cos_ref, sin_ref, dm_ref, xi_ref, zeta_ref, cd_ref, gain_ref,
                o_ref, s_ref, *, n_chunks, n_pad, dk, dv):
    s_ref[...] = jnp.zeros_like(s_ref)
    first_valid = jnp.where(lax.broadcasted_iota(I32, (CHUNK, 1), 0) >= n_pad, 1.0, 0.0)
    scale = dk ** -0.5
    half = dk // 2

    def rot(t, cos, sin):
        t1, t2 = t[:, :half], t[:, half:]
        return jnp.concatenate([t1 * cos - t2 * sin, t1 * sin + t2 * cos], axis=-1)

    def chunk(n, carry):
        sl = pl.ds(pl.multiple_of(n * CHUNK, CHUNK), CHUNK)
        valid = jnp.where(n == 0, first_valid, jnp.ones_like(first_valid))
        cos, sin = cos_ref[sl, :], sin_ref[sl, :]
        for hh in range(HEADS_PER_STEP):
            ck = slice(hh * dk, (hh + 1) * dk)
            cv = slice(hh * dv, (hh + 1) * dv)
            q = rot(q_ref[sl, ck].astype(F32), cos, sin) * valid
            k = rot(k_ref[sl, ck].astype(F32), cos, sin) * (scale * valid)
            vb = (v_ref[sl, cv].astype(F32) * valid).astype(BF16)
            scores = _dot_nt(q.astype(BF16), k.astype(BF16)) * dm_ref[hh]
            o = _dot(scores.astype(BF16), vb) + _dot((q * xi_ref[hh]).astype(BF16), s_ref[hh].astype(BF16))
            s_ref[hh] = s_ref[hh] * cd_ref[hh] + _dot_tn((k * zeta_ref[hh]).astype(BF16), vb)
            gt = gt_ref[sl, cv].astype(F32)
            o_ref[sl, cv] = (_head_norm_rows(o, gain_ref[:, cv]) * (gt * jax.nn.sigmoid(gt))).astype(o_ref.dtype)
        return carry

    lax.fori_loop(0, n_chunks, chunk, 0)


def _retention(proj3, norm_g, *, dk, dv):
    bsz, lp, _ = proj3.shape
    heads = RET_HEADS
    half = dk // 2
    pos_tok = jnp.arange(lp, dtype=F32) - float(CHUNK - N_META)
    inv = ROPE_BASE ** (-jnp.arange(0, dk, 2, dtype=F32) / dk)
    ang = pos_tok[:, None] * inv[None, :]
    cos_t, sin_t = jnp.cos(ang), jnp.sin(ang)
    log_gamma = jnp.log(1.0 - 2.0 ** (-5.0 - jnp.arange(heads, dtype=F32)))
    pos = jnp.arange(CHUNK, dtype=F32)
    diff = pos[:, None] - pos[None, :]
    dmask = jnp.where(diff >= 0, jnp.exp(log_gamma[:, None, None] * jnp.maximum(diff, 0.0)), 0.0)
    xi = jnp.exp(log_gamma[:, None] * (pos + 1.0))[:, :, None]
    zeta = jnp.exp(log_gamma[:, None] * (CHUNK - 1.0 - pos))[:, :, None]
    cdec = jnp.exp(log_gamma * CHUNK)[:, None, None]
    kern = functools.partial(_ret_kernel, n_chunks=lp // CHUNK, n_pad=CHUNK - N_META, dk=dk, dv=dv)
    hp = HEADS_PER_STEP
    wk, wv = hp * dk, hp * dv
    ng = heads // hp
    nv = (2 * heads * dk) // wv
    return pl.pallas_call(
        kern,
        out_shape=jax.ShapeDtypeStruct((bsz, lp, heads * dv), BF16),
        grid=(bsz, ng),
        in_specs=[pl.BlockSpec((None, lp, wk), lambda b, h: (b, 0, h)),
                  pl.BlockSpec((None, lp, wk), lambda b, h: (b, 0, ng + h)),
                  pl.BlockSpec((None, lp, wv), lambda b, h: (b, 0, nv + h)),
                  pl.BlockSpec((None, lp, wv), lambda b, h: (b, 0, nv + ng + h)),
                  pl.BlockSpec((lp, half), lambda b, h: (0, 0)),
                  pl.BlockSpec((lp, half), lambda b, h: (0, 0)),
                  pl.BlockSpec((hp, CHUNK, CHUNK), lambda b, h: (h, 0, 0)),
                  pl.BlockSpec((hp, CHUNK, 1), lambda b, h: (h, 0, 0)),
                  pl.BlockSpec((hp, CHUNK, 1), lambda b, h: (h, 0, 0)),
                  pl.BlockSpec((hp, 1, 1), lambda b, h: (h, 0, 0)),
                  pl.BlockSpec((1, wv), lambda b, h: (0, h))],
        out_specs=pl.BlockSpec((None, lp, wv), lambda b, h: (b, 0, h)),
        scratch_shapes=[pltpu.VMEM((hp, dk, dv), F32)],
        compiler_params=_cparams(("parallel", "parallel")),
        name="retention",
    )(proj3, proj3, proj3, proj3, cos_t, sin_t, dmask, xi, zeta, cdec, norm_g.reshape(1, heads * dv))


def _s5_prepare(log_dt, lam_re, lam_im, b_re, b_im, c_re, c_im, n_scan):
    cs = S5_CHUNK
    groups, states = lam_re.shape
    dt = jnp.exp(log_dt.astype(F32))[:, None]
    lr, li = lam_re.astype(F32), lam_im.astype(F32)
    mag = jnp.exp(lr * dt)
    ab_re, ab_im = mag * jnp.cos(li * dt), mag * jnp.sin(li * dt)
    den = lr * lr + li * li
    nr, ni = ab_re - 1.0, ab_im
    z_re = (nr * lr + ni * li) / den
    z_im = (ni * lr - nr * li) / den
    bz_re = z_re[..., None] * b_re - z_im[..., None] * b_im
    bz_im = z_re[..., None] * b_im + z_im[..., None] * b_re

    def power(kk):
        kk = kk.astype(F32)[..., None, None]
        m = jnp.exp(kk * (lr * dt))
        return m * jnp.cos(kk * (li * dt)), m * jnp.sin(kk * (li * dt))

    pw_re, pw_im = power(jnp.arange(cs + 1))
    cp_re = c_re[None] * pw_re[:, :, None, :] - c_im[None] * pw_im[:, :, None, :]
    cp_im = c_re[None] * pw_im[:, :, None, :] + c_im[None] * pw_re[:, :, None, :]
    hi = lax.Precision.HIGHEST
    kern = (jnp.einsum('tgcp,gpd->tgcd', cp_re, bz_re, precision=hi)
            - jnp.einsum('tgcp,gpd->tgcd', cp_im, bz_im, precision=hi))
    s_idx = jnp.arange(cs)[:, None]
    t_idx = jnp.arange(cs)[None, :]
    lag = jnp.clip(t_idx - s_idx, 0, cs - 1)
    toep = jnp.where((t_idx >= s_idx)[:, :, None, None, None], kern[lag], 0.0)
    toep = toep.transpose(2, 0, 4, 1, 3).reshape(groups, cs * S5_GROUP, cs * S5_GROUP)
    rev = pw_re[cs - 1 - jnp.arange(cs)], pw_im[cs - 1 - jnp.arange(cs)]
    win_re = rev[0][..., None] * bz_re[None] - rev[1][..., None] * bz_im[None]
    win_im = rev[0][..., None] * bz_im[None] + rev[1][..., None] * bz_re[None]
    win = jnp.concatenate([win_re, win_im], axis=2)
    win = win.transpose(1, 0, 3, 2).reshape(groups, cs * S5_GROUP, 2 * states)
    wout = jnp.concatenate([cp_re[1:], -cp_im[1:]], axis=3)
    wout = wout.transpose(1, 3, 0, 2).reshape(groups, 2 * states, cs * S5_GROUP)
    sc_re, sc_im = power(cs * (2 ** jnp.arange(n_scan)))
    mul_r = jnp.concatenate([sc_re, sc_re], axis=-1)
    mul_i = jnp.concatenate([-sc_im, sc_im], axis=-1)
    scan_mul = jnp.stack([mul_r, mul_i], axis=2).transpose(1, 0, 2, 3)
    return toep.astype(BF16), win.astype(BF16), wout.astype(BF16), scan_mul


def _s5_octets(toep, win, wout, scan_mul):
    groups = toep.shape[0]
    go = LANES // S5_GROUP
    no = groups // go
    cs, c = S5_CHUNK, S5_GROUP
    p = win.shape[-1] // 2
    n_scan = scan_mul.shape[1]
    eye = jnp.eye(go, dtype=toep.dtype)
    toep_o = jnp.einsum('ogsatc,gh->osgathc', toep.reshape(no, go, cs, c, cs, c), eye)
    win_o = jnp.einsum('ogsarp,gh->osgarhp', win.reshape(no, go, cs, c, 2, p), eye)
    wout_o = jnp.einsum('ogrptc,gh->orgpthc', wout.reshape(no, go, 2, p, cs, c), eye)
    mul_o = scan_mul.reshape(no, go, n_scan, 2, 2, p).transpose(0, 2, 3, 4, 1, 5)
    return (toep_o.reshape(no, cs * LANES, cs * LANES), win_o.reshape(no, cs * LANES, 2 * go * p),
            wout_o.reshape(no, 2 * go * p, cs * LANES), mul_o.reshape(no, n_scan, 2, 2 * go * p))


def _s5_kernel(u_ref, toep_ref, win_ref, wout_ref, mul_ref, y_ref, x_ref, start_ref, *,
               bsz, n_steps, n_scan, n_pad_steps):
    j = pl.program_id(1)
    rows = bsz * n_steps
    cs = S5_CHUNK

    @pl.when(j == 0)
    def _():
        step = lax.broadcasted_iota(I32, (rows, 1), 0)
        for bb in range(1, bsz):
            step = step - jnp.where(lax.broadcasted_iota(I32, (rows, 1), 0) >= bb * n_steps, n_steps, 0)
        keep = step >= n_pad_steps
        for s in range(cs):
            x_ref[:, s * LANES:(s + 1) * LANES] = jnp.where(keep, u_ref[:, s, :], 0.0).astype(BF16)
        acc = _dot(x_ref[...], win_ref[...])
        half = acc.shape[1] // 2
        for jj in range(n_scan):
            sh = 2 ** jj
            prev = jnp.where(step >= sh, pltpu.roll(acc, sh, 0), 0.0)
            swapped = pltpu.roll(prev, half, 1)
            acc = acc + mul_ref[jj, 0:1, :] * prev + mul_ref[jj, 1:2, :] * swapped
        start = jnp.where(step >= 1, pltpu.roll(acc, 1, 0), 0.0)
        start_ref[...] = start.astype(BF16)

    y = _dot(x_ref[...], toep_ref[...]) + _dot(start_ref[...], wout_ref[...])
    for tt in range(y_ref.shape[1]):
        y_ref[:, tt, :] = y[:, tt * LANES:(tt + 1) * LANES]


def _s5_scan(u3, toep, win, wout, scan_mul, *, bsz, n_steps, n_pad_steps):
    rows, cs, width = u3.shape
    no, k_in, k_st = win.shape
    n_scan = scan_mul.shape[1]
    t_half = cs // 2
    kern = functools.partial(_s5_kernel, bsz=bsz, n_steps=n_steps, n_scan=n_scan, n_pad_steps=n_pad_steps)
    return pl.pallas_call(
        kern,
        out_shape=jax.ShapeDtypeStruct((rows, cs, width), F32),
        grid=(no, 2),
        in_specs=[pl.BlockSpec((rows, cs, LANES), lambda o, j: (0, 0, o)),
                  pl.BlockSpec((None, k_in, t_half * LANES), lambda o, j: (o, 0, j)),
                  pl.BlockSpec((None, k_in, k_st), lambda o, j: (o, 0, 0)),
                  pl.BlockSpec((None, k_st, t_half * LANES), lambda o, j: (o, 0, j)),
                  pl.BlockSpec((None, n_scan, 2, k_st), lambda o, j: (o, 0, 0, 0))],
        out_specs=pl.BlockSpec((rows, t_half, LANES), lambda o, j: (0, j, o)),
        scratch_shapes=[pltpu.VMEM((rows, k_in), BF16), pltpu.VMEM((rows, k_st), BF16)],
        compiler_params=_cparams(("parallel", "arbitrary")),
        name="s5_scan",
    )(u3, toep, win, wout, scan_mul)


def _glu_kernel(y_ref, u_ref, d_ref, w_ref, o_ref):
    z = jax.nn.gelu(y_ref[...] + d_ref[...] * u_ref[...])
    o_ref[...] = (z * jax.nn.sigmoid(_dot(z.astype(BF16), w_ref[...]))).astype(o_ref.dtype)


def _s5_glu(y, proj, d, w_glu_bf):
    m, width = y.shape
    tm = _pick(m, (256, 128))
    return pl.pallas_call(
        _glu_kernel,
        out_shape=jax.ShapeDtypeStruct((m, width), BF16),
        grid=(m // tm,),
        in_specs=[pl.BlockSpec((tm, width), lambda i: (i, 0)),
                  pl.BlockSpec((tm, width), lambda i: (i, 0)),
                  pl.BlockSpec((1, width), lambda i: (0, 0)),
                  pl.BlockSpec((width, width), lambda i: (0, 0))],
        out_specs=pl.BlockSpec((tm, width), lambda i: (i, 0)),
        compiler_params=_cparams(("parallel",)),
        name="s5_glu",
    )(y, proj, d.reshape(1, width), w_glu_bf)


def _s5_mixer(proj, bsz, lp, params):
    log_dt, lam_re, lam_im, b_re, b_im, c_re, c_im, d, w_glu = params
    groups = lam_re.shape[0]
    width = groups * S5_GROUP
    cs = S5_CHUNK
    n_steps = lp // cs
    assert (bsz * n_steps) % 8 == 0, "super-step rows must fill whole sublane tiles"
    n_scan = max(1, math.ceil(math.log2(n_steps)))
    mats = _s5_octets(*_s5_prepare(log_dt, lam_re, lam_im, b_re, b_im, c_re, c_im, n_scan))
    y = _s5_scan(proj[:, :width].reshape(bsz * n_steps, cs, width), *mats, bsz=bsz, n_steps=n_steps,
                 n_pad_steps=(CHUNK - N_META) // cs)
    return _s5_glu(y.reshape(bsz * lp, width), proj, d, w_glu.astype(BF16))


def _router_kernel(x_ref, rt_ref, bias_ref, idx_ref, pos_ref, w_ref, cnt_ref, run_ref, *, tm):
    i = pl.program_id(0)

    @pl.when(i == 0)
    def _():
        run_ref[...] = jnp.zeros_like(run_ref)

    ne, ng = N_EXPERTS, N_GROUPS
    per = ne // ng
    neg = -jnp.inf
    logits = lax.dot_general(rt_ref[...], x_ref[...], (((1,), (1,)), ((), ())),
                             preferred_element_type=F32, precision=lax.Precision.HIGHEST)
    s = jax.nn.sigmoid(logits)
    sb = s + bias_ref[...]
    sb3 = sb.reshape(ng, per, tm)
    io = lax.broadcasted_iota(I32, (ng, per, tm), 1)
    m1 = jnp.max(sb3, axis=1, keepdims=True)
    first = jnp.min(jnp.where(sb3 == m1, io, per), axis=1, keepdims=True)
    m2 = jnp.max(jnp.where(io == first, neg, sb3), axis=1, keepdims=True)
    gs = m1 + m2
    gi = lax.broadcasted_iota(I32, (ng, 1, tm), 0)
    grank = jnp.zeros((ng, 1, tm), F32)
    for g2 in range(ng):
        other = gs[g2:g2 + 1]
        beats = (other > gs) | ((other == gs) & (g2 < gi))
        grank = grank + jnp.where(beats, 1.0, 0.0)
    gsel = grank < float(TOPK_GROUPS)
    ms = jnp.where(gsel, sb3, neg).reshape(ne, tm)
    ei = lax.broadcasted_iota(I32, (ne, tm), 0)
    rank = jnp.zeros((ne, tm), F32)
    for e2 in range(ne):
        other = ms[e2:e2 + 1, :]
        beats = (other > ms) | ((other == ms) & (e2 < ei))
        rank = rank + jnp.where(beats, 1.0, 0.0)
    sel = rank < float(TOP_K)
    self_ = jnp.where(sel, 1.0, 0.0)
    wsum = jnp.sum(jnp.where(sel, s, 0.0), axis=0, keepdims=True)
    wgt = s / wsum * ROUTE_SCALE
    ti = lax.broadcasted_iota(I32, (tm, tm), 0)
    tj = lax.broadcasted_iota(I32, (tm, tm), 1)
    upper = jnp.where(ti < tj, 1.0, 0.0).astype(BF16)
    pos = _dot(self_.astype(BF16), upper) + run_ref[:, 0:1]
    run_ref[...] = run_ref[...] + jnp.sum(self_, axis=1, keepdims=True)
    li = lax.broadcasted_iota(I32, (ne, ne), 0)
    lj = lax.broadcasted_iota(I32, (ne, ne), 1)
    lower = jnp.where(lj < li, 1.0, 0.0).astype(BF16)
    slot = _dot(lower, self_.astype(BF16))
    eif = ei.astype(F32)
    idx_rows, pos_rows, w_rows = [], [], []
    for kk in range(TOP_K):
        one = sel & (slot == float(kk))
        idx_rows.append(jnp.sum(jnp.where(one, eif, 0.0), axis=0, keepdims=True))
        pos_rows.append(jnp.sum(jnp.where(one, pos, 0.0), axis=0, keepdims=True))
        w_rows.append(jnp.sum(jnp.where(one, wgt, 0.0), axis=0, keepdims=True))
    idx_ref[...] = jnp.concatenate(idx_rows, axis=0).astype(I32)
    pos_ref[...] = jnp.concatenate(pos_rows, axis=0).astype(I32)
    w_ref[...] = jnp.concatenate(w_rows, axis=0)

    @pl.when(i == pl.num_programs(0) - 1)
    def _():
        cnt_ref[...] = run_ref[...].astype(I32)


def _router(h, router, bias):
    m, d = h.shape
    ne = router.shape[1]
    tm = _pick(m, (256, 128))
    kern = functools.partial(_router_kernel, tm=tm)
    return pl.pallas_call(
        kern,
        out_shape=(jax.ShapeDtypeStruct((TOP_K, m), I32), jax.ShapeDtypeStruct((TOP_K, m), I32),
                   jax.ShapeDtypeStruct((TOP_K, m), F32), jax.ShapeDtypeStruct((ne, LANES), I32)),
        grid=(m // tm,),
        in_specs=[pl.BlockSpec((tm, d), lambda i: (i, 0)),
                  pl.BlockSpec((ne, d), lambda i: (0, 0)),
                  pl.BlockSpec((ne, 1), lambda i: (0, 0))],
        out_specs=(pl.BlockSpec((TOP_K, tm), lambda i: (0, i)),
                   pl.BlockSpec((TOP_K, tm), lambda i: (0, i)),
                   pl.BlockSpec((TOP_K, tm), lambda i: (0, i)),
                   pl.BlockSpec((ne, LANES), lambda i: (0, 0))),
        scratch_shapes=[pltpu.VMEM((ne, LANES), F32)],
        compiler_params=_cparams(("arbitrary",)),
        name="moe_router",
    )(h, router.T, bias.reshape(ne, 1))


HI_HALF = 0xFFFF0000


def _pack_halves(y):
    half = y.shape[1] // 2
    bits = lax.bitcast_convert_type(y.astype(BF16).astype(F32), jnp.uint32)
    return (bits[:, :half] >> 16) | (bits[:, half:] & jnp.uint32(HI_HALF))


def _unpack_halves(w):
    return (lax.bitcast_convert_type(w << 16, F32),
            lax.bitcast_convert_type(w & jnp.uint32(HI_HALF), F32))


def _dispatch_kernel(lo_ref, hi_ref, dest_ref, x_ref, xs_ref, packed_ref, zero_ref, sem, *, tm):
    i = pl.program_id(0)

    def zero_copy(r):
        return pltpu.make_async_copy(zero_ref.at[pl.ds(0, 1)], xs_ref.at[pl.ds(r, 1)], sem)

    @pl.when(i == 0)
    def _():
        zero_ref[...] = jnp.zeros_like(zero_ref)

        def for_pad_rows(fn):
            def span(e, c):
                def row(r, cc):
                    fn(r)
                    return cc
                return lax.fori_loop(lo_ref[e], hi_ref[e], row, c)
            lax.fori_loop(0, lo_ref.shape[0], span, 0)

        for_pad_rows(lambda r: zero_copy(r).start())
        for_pad_rows(lambda r: zero_copy(r).wait())

    packed_ref[...] = _pack_halves(x_ref[...])

    def copy(r, kk):
        return pltpu.make_async_copy(packed_ref.at[pl.ds(r, 1)], xs_ref.at[pl.ds(dest_ref[kk, r], 1)], sem)

    def issue(r, c):
        for kk in range(TOP_K):
            copy(r, kk).start()
        return c

    def drain(r, c):
        for kk in range(TOP_K):
            copy(r, kk).wait()
        return c

    lax.fori_loop(0, tm, issue, 0)
    lax.fori_loop(0, tm, drain, 0)


def _dispatch(x, dest, pad_lo, pad_hi, n_rows):
    m, d = x.shape
    tm = _pick(m, (256, 128))
    kern = functools.partial(_dispatch_kernel, tm=tm)
    return pl.pallas_call(
        kern,
        out_shape=jax.ShapeDtypeStruct((n_rows, d // 2), jnp.uint32),
        grid_spec=pltpu.PrefetchScalarGridSpec(
            num_scalar_prefetch=2,
            grid=(m // tm,),
            in_specs=[pl.BlockSpec((TOP_K, tm), lambda i, lo, hi: (0, i), memory_space=pltpu.SMEM),
                      pl.BlockSpec((tm, d), lambda i, lo, hi: (i, 0))],
            out_specs=pl.BlockSpec(memory_space=pl.ANY),
            scratch_shapes=[pltpu.VMEM((tm, d // 2), jnp.uint32), pltpu.VMEM((8, d // 2), jnp.uint32),
                            pltpu.SemaphoreType.DMA]),
        compiler_params=_cparams(("arbitrary",)),
        name="moe_dispatch",
    )(pad_lo, pad_hi, dest, x)


def _ffn_kernel(be_ref, nb_ref, x_ref, wg_ref, wu_ref, wd_ref, y_ref, wgb_ref, wub_ref, wdb_ref, *, packed):
    i = pl.program_id(0)
    prev = be_ref[jnp.maximum(i - 1, 0)]
    fresh = (i == 0) | (be_ref[i] != prev)

    @pl.when(fresh & (i < nb_ref[0]))
    def _():
        wgb_ref[...] = wg_ref[...].astype(BF16)
        wub_ref[...] = wu_ref[...].astype(BF16)
        wdb_ref[...] = wd_ref[...].astype(BF16)

    @pl.when(i >= nb_ref[0])
    def _():
        y_ref[...] = jnp.zeros_like(y_ref)

    @pl.when(i < nb_ref[0])
    def _():
        if packed:
            lo, hi = _unpack_halves(x_ref[...])
            x = jnp.concatenate([lo.astype(BF16), hi.astype(BF16)], axis=1)
        else:
            x = x_ref[...]
        a = _dot(x, wgb_ref[...])
        u = _dot(x, wub_ref[...])
        hmid = (a * jax.nn.sigmoid(a)) * u
        y = _dot(hmid.astype(BF16), wdb_ref[...])
        y_ref[...] = _pack_halves(y) if packed else y


def _ffn(xs, block_e, n_active, layer, w_gate, w_up, w_down, tm, packed):
    n_rows = xs.shape[0]
    d, ff = w_gate.shape[-2:]
    nb = n_rows // tm
    width, out_dtype = (d // 2, jnp.uint32) if packed else (d, F32)

    def xmap(i, be, nbr):
        return (jnp.maximum(jnp.minimum(i, nbr[0] - 1), 0), 0)

    def wmap(i, be, nbr):
        return (layer, be[i], 0, 0)

    return pl.pallas_call(
        functools.partial(_ffn_kernel, packed=packed),
        out_shape=jax.ShapeDtypeStruct((n_rows, width), out_dtype),
        grid_spec=pltpu.PrefetchScalarGridSpec(
            num_scalar_prefetch=2,
            grid=(nb,),
            in_specs=[pl.BlockSpec((tm, width), xmap),
                      pl.BlockSpec((None, None, d, ff), wmap),
                      pl.BlockSpec((None, None, d, ff), wmap),
                      pl.BlockSpec((None, None, ff, d), wmap)],
            out_specs=pl.BlockSpec((tm, width), lambda i, be, nbr: (i, 0)),
            scratch_shapes=[pltpu.VMEM((d, ff), BF16), pltpu.VMEM((d, ff), BF16), pltpu.VMEM((ff, d), BF16)]),
        compiler_params=_cparams(("arbitrary",)),
        name="moe_ffn",
    )(block_e, n_active, xs, w_gate, w_up, w_down)


def _combine_kernel(dest_ref, ys_ref, w_ref, sh_ref, h_ref, g_ref, b_ref, o_ref, ob_ref, buf_ref, sem, *, tm):
    def copy(r, kk):
        return pltpu.make_async_copy(ys_ref.at[pl.ds(dest_ref[kk, r], 1)],
                                     buf_ref.at[kk, pl.ds(r, 1)], sem)

    def issue(r, c):
        for kk in range(TOP_K):
            copy(r, kk).start()
        return c

    def drain(r, c):
        for kk in range(TOP_K):
            copy(r, kk).wait()
        return c

    lax.fori_loop(0, tm, issue, 0)
    lax.fori_loop(0, tm, drain, 0)
    w = w_ref[...]
    lo, hi = _unpack_halves(buf_ref[0])
    r_lo, r_hi = w[:, 0:1] * lo, w[:, 0:1] * hi
    for kk in range(1, TOP_K):
        lo, hi = _unpack_halves(buf_ref[kk])
        r_lo, r_hi = r_lo + w[:, kk:kk + 1] * lo, r_hi + w[:, kk:kk + 1] * hi
    routed = jnp.concatenate([r_lo, r_hi], axis=1)
    y = DEEPNORM_ALPHA * h_ref[...] + (routed + sh_ref[...])
    out = _layer_norm_rows(y, g_ref[...], b_ref[...])
    o_ref[...] = out
    ob_ref[...] = out.astype(BF16)


def _combine_ln(ys, dest, w_t, shared, h, g, b):
    m, d = h.shape
    tm = _pick(m, (128,))
    kern = functools.partial(_combine_kernel, tm=tm)
    return pl.pallas_call(
        kern,
        out_shape=(jax.ShapeDtypeStruct((m, d), F32), jax.ShapeDtypeStruct((m, d), BF16)),
        grid=(m // tm,),
        in_specs=[pl.BlockSpec((TOP_K, tm), lambda i: (0, i), memory_space=pltpu.SMEM),
                  pl.BlockSpec(memory_space=pl.ANY),
                  pl.BlockSpec((tm, TOP_K), lambda i: (i, 0)),
                  pl.BlockSpec((tm, d), lambda i: (i, 0)),
                  pl.BlockSpec((tm, d), lambda i: (i, 0)),
                  pl.BlockSpec((1, d), lambda i: (0, 0)),
                  pl.BlockSpec((1, d), lambda i: (0, 0))],
        out_specs=(pl.BlockSpec((tm, d), lambda i: (i, 0)),
                   pl.BlockSpec((tm, d), lambda i: (i, 0))),
        scratch_shapes=[pltpu.VMEM((TOP_K, tm, d // 2), jnp.uint32), pltpu.SemaphoreType.DMA],
        compiler_params=_cparams(("arbitrary",)),
        name="moe_combine_ln",
    )(dest, ys, w_t, shared, h, g.reshape(1, d), b.reshape(1, d))


def _moe_ln(h, h_bf, layer, router, bias, w_gate, w_up, w_down, s_gate, s_up, s_down, ln_g, ln_b):
    m, d = h.shape
    tm = _pick(m, (256, 128))
    idx, pos, wsel, counts = _router(h, router, bias)
    counts = counts[:, 0]
    padded = ((counts + tm - 1) // tm) * tm
    pend = jnp.cumsum(padded)
    pstart = pend - padded
    experts = jnp.arange(N_EXPERTS, dtype=I32)
    dest = jnp.sum(jnp.where(idx[:, :, None] == experts, pstart, 0), axis=-1) + pos
    nb = (m * TOP_K) // tm + N_EXPERTS
    block_row = jnp.arange(nb, dtype=I32)[:, None] * tm
    block_e = jnp.minimum(jnp.sum((pend[None, :] <= block_row).astype(I32), axis=1), N_EXPERTS - 1)
    n_active = (pend[-1:] // tm).astype(I32)
    tail = jnp.full((1,), nb * tm, I32)
    xs = _dispatch(h, dest, jnp.concatenate([pstart + counts, pend[-1:]]), jnp.concatenate([pend, tail]), nb * tm)
    ys = _ffn(xs, block_e, n_active, layer, w_gate, w_up, w_down, tm, packed=True)
    shared = _ffn(h_bf, jnp.zeros((m // tm,), I32), jnp.full((1,), m // tm, I32), layer,
                  s_gate[:, None], s_up[:, None], s_down[:, None], tm, packed=False)
    return _combine_ln(ys, dest, wsel.T, shared, h, ln_g, ln_b)


def kernel(x, meta, ab_w_in, s5_log_dt, s5_lambda_re, s5_lambda_im, s5_b_re, s5_b_im, s5_c_re, s5_c_im, s5_d, s5_w_glu, gla_w_gate2, gla_b_gate, gla_norm_g, ab_w_out, ret_w_in, ret_norm_g, ret_w_out, ln1_g, ln1_b, ln2_g, ln2_b, moe_router, moe_bias, moe_w_gate, moe_w_up, moe_w_down, shared_w_gate, shared_w_up, shared_w_down):
    bsz, seq, d = x.shape
    length = seq + N_META
    pad = (-length) % CHUNK
    assert pad == CHUNK - N_META, "sequence length must be a multiple of the mixer chunk"
    lp = length + pad
    m = bsz * lp
    h3 = jnp.concatenate([jnp.zeros((bsz, pad, d), x.dtype),
                          jnp.broadcast_to(meta[None].astype(x.dtype), (bsz, N_META, d)), x], axis=1)
    h = h3.reshape(m, d)
    h_bf = h.astype(BF16)
    depth = ln1_g.shape[0]
    for layer in range(depth):
        i = layer // 2
        if layer % 2 == 0:
            s5_w = s5_lambda_re.shape[1] * S5_GROUP
            gla_qk = gla_w_gate2.shape[2]
            gla_w = gla_norm_g.shape[1]
            n_main = s5_w + 2 * gla_qk + 2 * gla_w
            w_in = ab_w_in[i].astype(BF16)
            u = _matmul(h_bf, w_in, 0, s5_w, F32)
            proj = _matmul(h_bf, w_in, s5_w, n_main - s5_w, BF16)
            gate = _gla_gate(h_bf, ab_w_in[i][:, n_main:], gla_w_gate2[i], gla_b_gate[i])
            y_a = _s5_mixer(u, bsz, lp, (s5_log_dt[i], s5_lambda_re[i], s5_lambda_im[i], s5_b_re[i], s5_b_im[i],
                                         s5_c_re[i], s5_c_im[i], s5_d[i], s5_w_glu[i]))
            o = _gla(proj.reshape(bsz, lp, n_main - s5_w), gate.reshape(bsz, lp, gla_qk), gla_norm_g[i],
                     q_off=0, k_off=gla_qk, v_off=2 * gla_qk, r_off=2 * gla_qk + gla_w,
                     dk=gla_qk // GLA_HEADS, dv=gla_w // GLA_HEADS)
            mixed = jnp.concatenate([y_a, o.reshape(m, gla_w)], axis=1)
            w_out = ab_w_out[i].astype(BF16)
        else:
            ret_w = ret_norm_g.shape[1]
            ret_qk = (ret_w_in.shape[2] - 2 * ret_w) // 2
            proj = _matmul(h_bf, ret_w_in[i].astype(BF16), 0, ret_w_in.shape[2], BF16)
            o = _retention(proj.reshape(bsz, lp, ret_w_in.shape[2]), ret_norm_g[i],
                           dk=ret_qk // RET_HEADS, dv=ret_w // RET_HEADS)
            mixed = o.reshape(m, ret_w)
            w_out = ret_w_out[i].astype(BF16)
        h, h_bf = _matmul_res_ln(mixed, w_out, h, ln1_g[layer], ln1_b[layer])
        h, h_bf = _moe_ln(h, h_bf, layer, moe_router[layer], moe_bias[layer], moe_w_gate, moe_w_up, moe_w_down,
                          shared_w_gate, shared_w_up, shared_w_down, ln2_g[layer], ln2_b[layer])
    return h.reshape(bsz, lp, d)[:, pad + N_META:]
```

```python
import functools
import math

import jax
import jax.numpy as jnp
import numpy as np
from jax import lax
from jax.experimental import pallas as pl
from jax.experimental.pallas import tpu as pltpu

F32 = jnp.float32
BF16 = jnp.bfloat16
I32 = jnp.int32

N_META = 16
CHUNK = 64
LN_EPS = 1e-5
S5_GROUP = 16
S5_STATE = 64
S5_CHUNK = 16
GLA_HEADS = 4
GLA_RANK = 16
GLA_TAU = 16.0
RET_HEADS = 8
ROPE_BASE = 10000.0
N_EXPERTS = 64
TOP_K = 8
N_GROUPS = 8
TOPK_GROUPS = 4
ROUTE_SCALE = 2.5
DEPTH = 2
DEEPNORM_ALPHA = (2 * DEPTH) ** 0.25

LANES = 128
VMEM_LIMIT = 56 * 1024 * 1024


def _cparams(sem):
    return pltpu.CompilerParams(dimension_semantics=sem, vmem_limit_bytes=VMEM_LIMIT)


def _pick(n, cands):
    for c in cands:
        if n % c == 0:
            return c
    raise ValueError(f"no tile for {n} in {cands}")


def _dot(a, b):
    return jnp.dot(a, b, preferred_element_type=F32)


def _dot_nt(a, b):
    return lax.dot_general(a, b, (((1,), (1,)), ((), ())), preferred_element_type=F32)


def _dot_tn(a, b):
    return lax.dot_general(a, b, (((0,), (0,)), ((), ())), preferred_element_type=F32)


def _dot_hi(a, b):
    return jnp.dot(a, b, preferred_element_type=F32, precision=lax.Precision.HIGHEST)


def _mm_kernel(a_ref, w_ref, o_ref):
    o_ref[...] = _dot(a_ref[...], w_ref[...]).astype(o_ref.dtype)


def _matmul(a, w, col0, n_cols, out_dtype):
    m, kdim = a.shape
    tm = _pick(m, (1408, 768, 384, 128))
    tn = _pick(math.gcd(n_cols, col0) if col0 else n_cols, (512, 256, 128))
    j0 = col0 // tn
    return pl.pallas_call(
        _mm_kernel,
        out_shape=jax.ShapeDtypeStruct((m, n_cols), out_dtype),
        grid=(m // tm, n_cols // tn),
        in_specs=[pl.BlockSpec((tm, kdim), lambda i, j: (i, 0)),
                  pl.BlockSpec((kdim, tn), lambda i, j: (0, j0 + j))],
        out_specs=pl.BlockSpec((tm, tn), lambda i, j: (i, j)),
        compiler_params=_cparams(("parallel", "arbitrary")),
        name="matmul",
    )(a, w)


def _layer_norm_rows(y, g, b):
    mu = jnp.mean(y, axis=-1, keepdims=True)
    var = jnp.mean(jnp.square(y - mu), axis=-1, keepdims=True)
    return (y - mu) * lax.rsqrt(var + LN_EPS) * g + b


def _mm_ln_kernel(a_ref, w_ref, h_ref, g_ref, b_ref, o_ref, ob_ref, acc_ref):
    k = pl.program_id(1)

    @pl.when(k == 0)
    def _():
        acc_ref[...] = jnp.zeros_like(acc_ref)

    acc_ref[...] += _dot(a_ref[...], w_ref[...])

    @pl.when(k == pl.num_programs(1) - 1)
    def _():
        y = DEEPNORM_ALPHA * h_ref[...] + acc_ref[...]
        out = _layer_norm_rows(y, g_ref[...], b_ref[...])
        o_ref[...] = out
        ob_ref[...] = out.astype(BF16)


def _matmul_res_ln(a, w, h, g, b):
    m, kdim = a.shape
    d = w.shape[1]
    tm = _pick(m, (384, 128))
    tk = _pick(kdim, (512, 256, 128))
    return pl.pallas_call(
        _mm_ln_kernel,
        out_shape=(jax.ShapeDtypeStruct((m, d), F32), jax.ShapeDtypeStruct((m, d), BF16)),
        grid=(m // tm, kdim // tk),
        in_specs=[pl.BlockSpec((tm, tk), lambda i, k: (i, k)),
                  pl.BlockSpec((tk, d), lambda i, k: (k, 0)),
                  pl.BlockSpec((tm, d), lambda i, k: (i, 0)),
                  pl.BlockSpec((1, d), lambda i, k: (0, 0)),
                  pl.BlockSpec((1, d), lambda i, k: (0, 0))],
        out_specs=(pl.BlockSpec((tm, d), lambda i, k: (i, 0)),
                   pl.BlockSpec((tm, d), lambda i, k: (i, 0))),
        scratch_shapes=[pltpu.VMEM((tm, d), F32)],
        compiler_params=_cparams(("parallel", "arbitrary")),
        name="matmul_res_ln",
    )(a, w, h, g.reshape(1, d), b.reshape(1, d))


def _log_sigmoid(x):
    return jnp.minimum(x, 0.0) - jnp.log1p(jnp.exp(-jnp.abs(x)))


def _gate_kernel(h_ref, wl_ref, w2_ref, b_ref, o_ref):
    low = _dot(h_ref[...], wl_ref[...])
    pre = _dot(low.astype(BF16), w2_ref[...]) + b_ref[...]
    o_ref[...] = _log_sigmoid(pre) / GLA_TAU


def _gla_gate(h_bf, w_low, w_gate2, b_gate):
    m, d = h_bf.shape
    qk = w_gate2.shape[1]
    tm = _pick(m, (256, 128))
    wl = jnp.zeros((d, LANES), BF16).at[:, :GLA_RANK].set(w_low.astype(BF16))
    w2 = jnp.zeros((LANES, qk), BF16).at[:GLA_RANK].set(w_gate2.astype(BF16))
    return pl.pallas_call(
        _gate_kernel,
        out_shape=jax.ShapeDtypeStruct((m, qk), F32),
        grid=(m // tm,),
        in_specs=[pl.BlockSpec((tm, d), lambda i: (i, 0)),
                  pl.BlockSpec((d, LANES), lambda i: (0, 0)),
                  pl.BlockSpec((LANES, qk), lambda i: (0, 0)),
                  pl.BlockSpec((1, qk), lambda i: (0, 0))],
        out_specs=pl.BlockSpec((tm, qk), lambda i: (i, 0)),
        compiler_params=_cparams(("parallel",)),
        name="gla_gate",
    )(h_bf, wl, w2, b_gate.reshape(1, qk))


def _head_norm_rows(o, gain):
    mu = jnp.mean(o, axis=-1, keepdims=True)
    var = jnp.mean(jnp.square(o - mu), axis=-1, keepdims=True)
    return (o - mu) * lax.rsqrt(var + LN_EPS) * gain


HEADS_PER_STEP = 2


def _gla_kernel(q_ref, k_ref, v_ref, r_ref, g_ref, gain_ref, o_ref, s_ref, *, n_chunks, n_pad, dk, dv):
    s_ref[...] = jnp.zeros_like(s_ref)
    row = lax.broadcasted_iota(I32, (CHUNK, CHUNK), 0)
    col = lax.broadcasted_iota(I32, (CHUNK, CHUNK), 1)
    tril = jnp.where(row >= col, 1.0, 0.0).astype(F32)
    first_valid = jnp.where(lax.broadcasted_iota(I32, (CHUNK, 1), 0) >= n_pad, 1.0, 0.0)
    scale = dk ** -0.5

    def chunk(n, carry):
        sl = pl.ds(pl.multiple_of(n * CHUNK, CHUNK), CHUNK)
        valid = jnp.where(n == 0, first_valid, jnp.ones_like(first_valid))
        for hh in range(HEADS_PER_STEP):
            ck = slice(hh * dk, (hh + 1) * dk)
            cv = slice(hh * dv, (hh + 1) * dv)
            q = q_ref[sl, ck].astype(F32) * scale * valid
            k = k_ref[sl, ck].astype(F32) * valid
            vb = (v_ref[sl, cv].astype(F32) * valid).astype(BF16)
            g = g_ref[sl, ck] * valid
            b = _dot_hi(tril, g)
            b_last = b[CHUNK - 1:CHUNK, :]
            q_in = (q * jnp.exp(b)).astype(BF16)
            k_out = k * jnp.exp(-b)
            k_end = k * jnp.exp(b_last - b)
            scores = _dot_nt(q_in, k_out.astype(BF16)) * tril
            o = _dot(scores.astype(BF16), vb) + _dot_nt(q_in, s_ref[hh].astype(BF16))
            s_ref[hh] = s_ref[hh] * jnp.exp(b_last) + _dot_tn(vb, k_end.astype(BF16))
            r = r_ref[sl, cv].astype(F32)
            o_ref[sl, cv] = (_head_norm_rows(o, gain_ref[:, cv]) * (r * jax.nn.sigmoid(r))).astype(o_ref.dtype)
        return carry

    lax.fori_loop(0, n_chunks, chunk, 0)


def _gla(proj3, gate3, norm_g, *, q_off, k_off, v_off, r_off, dk, dv):
    bsz, lp, _ = proj3.shape
    heads = GLA_HEADS
    hp = HEADS_PER_STEP
    wk, wv = hp * dk, hp * dv
    kern = functools.partial(_gla_kernel, n_chunks=lp // CHUNK, n_pad=CHUNK - N_META, dk=dk, dv=dv)
    return pl.pallas_call(
        kern,
        out_shape=jax.ShapeDtypeStruct((bsz, lp, heads * dv), BF16),
        grid=(bsz, heads // hp),
        in_specs=[pl.BlockSpec((None, lp, wk), lambda b, h: (b, 0, q_off // wk + h)),
                  pl.BlockSpec((None, lp, wk), lambda b, h: (b, 0, k_off // wk + h)),
                  pl.BlockSpec((None, lp, wv), lambda b, h: (b, 0, v_off // wv + h)),
                  pl.BlockSpec((None, lp, wv), lambda b, h: (b, 0, r_off // wv + h)),
                  pl.BlockSpec((None, lp, wk), lambda b, h: (b, 0, h)),
                  pl.BlockSpec((1, wv), lambda b, h: (0, h))],
        out_specs=pl.BlockSpec((None, lp, wv), lambda b, h: (b, 0, h)),
        scratch_shapes=[pltpu.VMEM((hp, dv, dk), F32)],
        compiler_params=_cparams(("parallel", "parallel")),
        name="gla",
    )(proj3, proj3, proj3, proj3, gate3, norm_g.reshape(1, heads * dv))


def _ret_kernel(q_ref, k_ref, v_ref, gt_ref, cos_ref, sin_ref, dm_ref, xi_ref, zeta_ref, cd_ref, gain_ref,
                o_ref, s_ref, *, n_chunks, n_pad, dk, dv):
    s_ref[...] = jnp.zeros_like(s_ref)
    first_valid = jnp.where(lax.broadcasted_iota(I32, (CHUNK, 1), 0) >= n_pad, 1.0, 0.0)
    scale = dk ** -0.5
    half = dk // 2

    def rot(t, cos, sin):
        t1, t2 = t[:, :half], t[:, half:]
        return jnp.concatenate([t1 * cos - t2 * sin, t1 * sin + t2 * cos], axis=-1)

    def chunk(n, carry):
        sl = pl.ds(pl.multiple_of(n * CHUNK, CHUNK), CHUNK)
        valid = jnp.where(n == 0, first_valid, jnp.ones_like(first_valid))
        cos, sin = cos_ref[sl, :], sin_ref[sl, :]
        for hh in range(HEADS_PER_STEP):
            ck = slice(hh * dk, (hh + 1) * dk)
            cv = slice(hh * dv, (hh + 1) * dv)
            q = rot(q_ref[sl, ck].astype(F32), cos, sin) * valid
            k = rot(k_ref[sl, ck].astype(F32), cos, sin) * (scale * valid)
            vb = (v_ref[sl, cv].astype(F32) * valid).astype(BF16)
            scores = _dot_nt(q.astype(BF16), k.astype(BF16)) * dm_ref[hh]
            o = _dot(scores.astype(BF16), vb) + _dot((q * xi_ref[hh]).astype(BF16), s_ref[hh].astype(BF16))
            s_ref[hh] = s_ref[hh] * cd_ref[hh] + _dot_tn((k * zeta_ref[hh]).astype(BF16), vb)
            gt = gt_ref[sl, cv].astype(F32)
            o_ref[sl, cv] = (_head_norm_rows(o, gain_ref[:, cv]) * (gt * jax.nn.sigmoid(gt))).astype(o_ref.dtype)
        return carry

    lax.fori_loop(0, n_chunks, chunk, 0)


def _retention(proj3, norm_g, *, dk, dv):
    bsz, lp, _ = proj3.shape
    heads = RET_HEADS
    half = dk // 2
    pos_tok = jnp.arange(lp, dtype=F32) - float(CHUNK - N_META)
    inv = ROPE_BASE ** (-jnp.arange(0, dk, 2, dtype=F32) / dk)
    ang = pos_tok[:, None] * inv[None, :]
    cos_t, sin_t = jnp.cos(ang), jnp.sin(ang)
    log_gamma = jnp.log(1.0 - 2.0 ** (-5.0 - jnp.arange(heads, dtype=F32)))
    pos = jnp.arange(CHUNK, dtype=F32)
    diff = pos[:, None] - pos[None, :]
    dmask = jnp.where(diff >= 0, jnp.exp(log_gamma[:, None, None] * jnp.maximum(diff, 0.0)), 0.0)
    xi = jnp.exp(log_gamma[:, None] * (pos + 1.0))[:, :, None]
    zeta = jnp.exp(log_gamma[:, None] * (CHUNK - 1.0 - pos))[:, :, None]
    cdec = jnp.exp(log_gamma * CHUNK)[:, None, None]
    kern = functools.partial(_ret_kernel, n_chunks=lp // CHUNK, n_pad=CHUNK - N_META, dk=dk, dv=dv)
    hp = HEADS_PER_STEP
    wk, wv = hp * dk, hp * dv
    ng = heads // hp
    nv = (2 * heads * dk) // wv
    return pl.pallas_call(
        kern,
        out_shape=jax.ShapeDtypeStruct((bsz, lp, heads * dv), BF16),
        grid=(bsz, ng),
        in_specs=[pl.BlockSpec((None, lp, wk), lambda b, h: (b, 0, h)),
                  pl.BlockSpec((None, lp, wk), lambda b, h: (b, 0, ng + h)),
                  pl.BlockSpec((None, lp, wv), lambda b, h: (b, 0, nv + h)),
                  pl.BlockSpec((None, lp, wv), lambda b, h: (b, 0, nv + ng + h)),
                  pl.BlockSpec((lp, half), lambda b, h: (0, 0)),
                  pl.BlockSpec((lp, half), lambda b, h: (0, 0)),
                  pl.BlockSpec((hp, CHUNK, CHUNK), lambda b, h: (h, 0, 0)),
                  pl.BlockSpec((hp, CHUNK, 1), lambda b, h: (h, 0, 0)),
                  pl.BlockSpec((hp, CHUNK, 1), lambda b, h: (h, 0, 0)),
                  pl.BlockSpec((hp, 1, 1), lambda b, h: (h, 0, 0)),
                  pl.BlockSpec((1, wv), lambda b, h: (0, h))],
        out_specs=pl.BlockSpec((None, lp, wv), lambda b, h: (b, 0, h)),
        scratch_shapes=[pltpu.VMEM((hp, dk, dv), F32)],
        compiler_params=_cparams(("parallel", "parallel")),
        name="retention",
    )(proj3, proj3, proj3, proj3, cos_t, sin_t, dmask, xi, zeta, cdec, norm_g.reshape(1, heads * dv))


def _s5_prepare(log_dt, lam_re, lam_im, b_re, b_im, c_re, c_im, n_scan):
    cs = S5_CHUNK
    groups, states = lam_re.shape
    dt = jnp.exp(log_dt.astype(F32))[:, None]
    lr, li = lam_re.astype(F32), lam_im.astype(F32)
    mag = jnp.exp(lr * dt)
    ab_re, ab_im = mag * jnp.cos(li * dt), mag * jnp.sin(li * dt)
    den = lr * lr + li * li
    nr, ni = ab_re - 1.0, ab_im
    z_re = (nr * lr + ni * li) / den
    z_im = (ni * lr - nr * li) / den
    bz_re = z_re[..., None] * b_re - z_im[..., None] * b_im
    bz_im = z_re[..., None] * b_im + z_im[..., None] * b_re

    def power(kk):
        kk = kk.astype(F32)[..., None, None]
        m = jnp.exp(kk * (lr * dt))
        return m * jnp.cos(kk * (li * dt)), m * jnp.sin(kk * (li * dt))

    pw_re, pw_im = power(jnp.arange(cs + 1))
    cp_re = c_re[None] * pw_re[:, :, None, :] - c_im[None] * pw_im[:, :, None, :]
    cp_im = c_re[None] * pw_im[:, :, None, :] + c_im[None] * pw_re[:, :, None, :]
    hi = lax.Precision.HIGHEST
    kern = (jnp.einsum('tgcp,gpd->tgcd', cp_re, bz_re, precision=hi)
            - jnp.einsum('tgcp,gpd->tgcd', cp_im, bz_im, precision=hi))
    rev = pw_re[cs - 1 - jnp.arange(cs)], pw_im[cs - 1 - jnp.arange(cs)]
    win_re = rev[0][..., None] * bz_re[None] - rev[1][..., None] * bz_im[None]
    win_im = rev[0][..., None] * bz_im[None] + rev[1][..., None] * bz_re[None]
    win = jnp.concatenate([win_re, win_im], axis=2)
    win = win.transpose(1, 0, 3, 2).reshape(groups, cs * S5_GROUP, 2 * states)
    wout = jnp.concatenate([cp_re[1:], -cp_im[1:]], axis=3)
    wout = wout.transpose(1, 3, 0, 2).reshape(groups, 2 * states, cs * S5_GROUP)
    sc_re, sc_im = power(cs * (2 ** jnp.arange(n_scan)))
    mul_r = jnp.concatenate([sc_re, sc_re], axis=-1)
    mul_i = jnp.concatenate([-sc_im, sc_im], axis=-1)
    scan_mul = jnp.stack([mul_r, mul_i], axis=2).transpose(1, 0, 2, 3)
    return kern[:cs].astype(BF16), win.astype(BF16), wout.astype(BF16), scan_mul


def _s5_octets(kern, win, wout, scan_mul):
    cs, c = S5_CHUNK, S5_GROUP
    groups = kern.shape[1]
    go = LANES // c
    no = groups // go
    st = win.shape[-1]
    n_scan = scan_mul.shape[1]
    lane = jnp.arange(LANES)
    spread = (lane[None, :] % c == jnp.arange(c)[:, None]).astype(BF16)
    kt = kern.reshape(cs, no, go, c, c).transpose(1, 0, 2, 4, 3).reshape(no, cs, LANES, c)
    lag_blocks = jnp.where(lane[:, None] // c == lane[None, :] // c,
                           jnp.einsum('olic,cj->olij', kt, spread, preferred_element_type=F32), 0.0)
    rows_in = jnp.arange(cs * LANES)
    w2 = win.reshape(no, go, cs, c, st).transpose(0, 2, 1, 3, 4).reshape(no, cs * LANES, st)
    win_o = jnp.where(((rows_in % LANES) // c)[:, None] == jnp.arange(go * st)[None, :] // st,
                      jnp.tile(w2, (1, 1, go)), 0.0)
    cols_out = jnp.arange(cs * LANES)
    src = jnp.arange(cs * c)
    spread_t = ((src[:, None] // c == cols_out[None, :] // LANES)
                & (src[:, None] % c == cols_out[None, :] % c)).astype(BF16)
    wout_o = jnp.where(jnp.arange(go * st)[:, None] // st == ((cols_out % LANES) // c)[None, :],
                       jnp.einsum('oik,kj->oij', wout.reshape(no, go * st, cs * c), spread_t,
                                  preferred_element_type=F32), 0.0)
    mul_o = scan_mul.reshape(no, go, n_scan, 2, st).transpose(0, 2, 3, 1, 4).reshape(no, n_scan, 2, go * st)
    return lag_blocks.astype(BF16), win_o.astype(BF16), wout_o.astype(BF16), mul_o


def _s5_kernel(u_ref, lag_ref, win_ref, wout_ref, mul_ref, y_ref, x_ref, start_ref, toep_ref, *,
               bsz, n_steps, n_scan, n_pad_steps):
    j = pl.program_id(1)
    rows = bsz * n_steps
    cs = S5_CHUNK
    t_half = y_ref.shape[1]

    for jv in range(cs // t_half):
        @pl.when(j == jv)
        def _(jv=jv):
            for s in range(cs):
                for tt in range(t_half):
                    lag = jv * t_half + tt - s
                    blk = lag_ref[lag] if lag >= 0 else jnp.zeros((LANES, LANES), BF16)
                    toep_ref[s * LANES:(s + 1) * LANES, tt * LANES:(tt + 1) * LANES] = blk

    @pl.when(j == 0)
    def _():
        step = lax.broadcasted_iota(I32, (rows, 1), 0)
        for bb in range(1, bsz):
            step = step - jnp.where(lax.broadcasted_iota(I32, (rows, 1), 0) >= bb * n_steps, n_steps, 0)
        keep = step >= n_pad_steps
        for s in range(cs):
            x_ref[:, s * LANES:(s + 1) * LANES] = jnp.where(keep, u_ref[:, s, :], 0.0).astype(BF16)
        acc = _dot(x_ref[...], win_ref[...])
        k_st = acc.shape[1]
        re_lane = (lax.broadcasted_iota(I32, (1, k_st), 1) & (LANES - 1)) < LANES // 2
        for jj in range(n_scan):
            sh = 2 ** jj
            prev = jnp.where(step >= sh, pltpu.roll(acc, sh, 0), 0.0)
            swapped = jnp.where(re_lane, pltpu.roll(prev, k_st - LANES // 2, 1), pltpu.roll(prev, LANES // 2, 1))
            acc = acc + mul_ref[jj, 0:1, :] * prev + mul_ref[jj, 1:2, :] * swapped
        start = jnp.where(step >= 1, pltpu.roll(acc, 1, 0), 0.0)
        start_ref[...] = start.astype(BF16)

    y = _dot(x_ref[...], toep_ref[...]) + _dot(start_ref[...], wout_ref[...])
    for tt in range(t_half):
        y_ref[:, tt, :] = y[:, tt * LANES:(tt + 1) * LANES]


def _s5_scan(u3, lag_blocks, win, wout, scan_mul, *, bsz, n_steps, n_pad_steps):
    rows, cs, width = u3.shape
    no, k_in, k_st = win.shape
    n_scan = scan_mul.shape[1]
    t_half = cs // 2
    kern = functools.partial(_s5_kernel, bsz=bsz, n_steps=n_steps, n_scan=n_scan, n_pad_steps=n_pad_steps)
    return pl.pallas_call(
        kern,
        out_shape=jax.ShapeDtypeStruct((rows, cs, width), F32),
        grid=(no, cs // t_half),
        in_specs=[pl.BlockSpec((rows, cs, LANES), lambda o, j: (0, 0, o)),
                  pl.BlockSpec((None, cs, LANES, LANES), lambda o, j: (o, 0, 0, 0)),
                  pl.BlockSpec((None, k_in, k_st), lambda o, j: (o, 0, 0)),
                  pl.BlockSpec((None, k_st, t_half * LANES), lambda o, j: (o, 0, j)),
                  pl.BlockSpec((None, n_scan, 2, k_st), lambda o, j: (o, 0, 0, 0))],
        out_specs=pl.BlockSpec((rows, t_half, LANES), lambda o, j: (0, j, o)),
        scratch_shapes=[pltpu.VMEM((rows, k_in), BF16), pltpu.VMEM((rows, k_st), BF16),
                        pltpu.VMEM((k_in, t_half * LANES), BF16)],
        compiler_params=_cparams(("parallel", "arbitrary")),
        name="s5_scan",
    )(u3, lag_blocks, win, wout, scan_mul)


def _glu_kernel(y_ref, u_ref, d_ref, w_ref, o_ref):
    z = jax.nn.gelu(y_ref[...] + d_ref[...] * u_ref[...])
    o_ref[...] = (z * jax.nn.sigmoid(_dot(z.astype(BF16), w_ref[...]))).astype(o_ref.dtype)


def _s5_glu(y, proj, d, w_glu_bf):
    m, width = y.shape
    tm = _pick(m, (256, 128))
    return pl.pallas_call(
        _glu_kernel,
        out_shape=jax.ShapeDtypeStruct((m, width), BF16),
        grid=(m // tm,),
        in_specs=[pl.BlockSpec((tm, width), lambda i: (i, 0)),
                  pl.BlockSpec((tm, width), lambda i: (i, 0)),
                  pl.BlockSpec((1, width), lambda i: (0, 0)),
                  pl.BlockSpec((width, width), lambda i: (0, 0))],
        out_specs=pl.BlockSpec((tm, width), lambda i: (i, 0)),
        compiler_params=_cparams(("parallel",)),
        name="s5_glu",
    )(y, proj, d.reshape(1, width), w_glu_bf)


def _s5_mixer(proj, bsz, lp, params):
    log_dt, lam_re, lam_im, b_re, b_im, c_re, c_im, d, w_glu = params
    groups = lam_re.shape[0]
    width = groups * S5_GROUP
    cs = S5_CHUNK
    n_steps = lp // cs
    assert (bsz * n_steps) % 8 == 0, "super-step rows must fill whole sublane tiles"
    n_scan = max(1, math.ceil(math.log2(n_steps)))
    mats = _s5_octets(*_s5_prepare(log_dt, lam_re, lam_im, b_re, b_im, c_re, c_im, n_scan))
    y = _s5_scan(proj[:, :width].reshape(bsz * n_steps, cs, width), *mats, bsz=bsz, n_steps=n_steps,
                 n_pad_steps=(CHUNK - N_META) // cs)
    return _s5_glu(y.reshape(bsz * lp, width), proj, d, w_glu.astype(BF16))


def _router_kernel(x_ref, rt_ref, bias_ref, idx_ref, pos_ref, w_ref, cnt_ref, run_ref, *, tm):
    i = pl.program_id(0)

    @pl.when(i == 0)
    def _():
        run_ref[...] = jnp.zeros_like(run_ref)

    ne, ng = N_EXPERTS, N_GROUPS
    per = ne // ng
    neg = -jnp.inf
    logits = lax.dot_general(rt_ref[...], x_ref[...], (((1,), (1,)), ((), ())),
                             preferred_element_type=F32, precision=lax.Precision.HIGHEST)
    s = jax.nn.sigmoid(logits)
    sb = s + bias_ref[...]
    sb3 = sb.reshape(ng, per, tm)
    io = lax.broadcasted_iota(I32, (ng, per, tm), 1)
    m1 = jnp.max(sb3, axis=1, keepdims=True)
    first = jnp.min(jnp.where(sb3 == m1, io, per), axis=1, keepdims=True)
    m2 = jnp.max(jnp.where(io == first, neg, sb3), axis=1, keepdims=True)
    gs = m1 + m2
    gi = lax.broadcasted_iota(I32, (ng, 1, tm), 0)
    grank = jnp.zeros((ng, 1, tm), F32)
    for g2 in range(ng):
        other = gs[g2:g2 + 1]
        beats = (other > gs) | ((other == gs) & (g2 < gi))
        grank = grank + jnp.where(beats, 1.0, 0.0)
    gsel = grank < float(TOPK_GROUPS)
    ms = jnp.where(gsel, sb3, neg).reshape(ne, tm)
    ei = lax.broadcasted_iota(I32, (ne, tm), 0)
    rank = jnp.zeros((ne, tm), F32)
    for e2 in range(ne):
        other = ms[e2:e2 + 1, :]
        beats = (other > ms) | ((other == ms) & (e2 < ei))
        rank = rank + jnp.where(beats, 1.0, 0.0)
    sel = rank < float(TOP_K)
    self_ = jnp.where(sel, 1.0, 0.0)
    wsum = jnp.sum(jnp.where(sel, s, 0.0), axis=0, keepdims=True)
    wgt = s / wsum * ROUTE_SCALE
    ti = lax.broadcasted_iota(I32, (tm, tm), 0)
    tj = lax.broadcasted_iota(I32, (tm, tm), 1)
    upper = jnp.where(ti < tj, 1.0, 0.0).astype(BF16)
    pos = _dot(self_.astype(BF16), upper) + run_ref[:, 0:1]
    run_ref[...] = run_ref[...] + jnp.sum(self_, axis=1, keepdims=True)
    li = lax.broadcasted_iota(I32, (ne, ne), 0)
    lj = lax.broadcasted_iota(I32, (ne, ne), 1)
    lower = jnp.where(lj < li, 1.0, 0.0).astype(BF16)
    slot = _dot(lower, self_.astype(BF16))
    eif = ei.astype(F32)
    idx_rows, pos_rows, w_rows = [], [], []
    for kk in range(TOP_K):
        one = sel & (slot == float(kk))
        idx_rows.append(jnp.sum(jnp.where(one, eif, 0.0), axis=0, keepdims=True))
        pos_rows.append(jnp.sum(jnp.where(one, pos, 0.0), axis=0, keepdims=True))
        w_rows.append(jnp.sum(jnp.where(one, wgt, 0.0), axis=0, keepdims=True))
    idx_ref[...] = jnp.concatenate(idx_rows, axis=0).astype(I32)
    pos_ref[...] = jnp.concatenate(pos_rows, axis=0).astype(I32)
    w_ref[...] = jnp.concatenate(w_rows, axis=0)

    @pl.when(i == pl.num_programs(0) - 1)
    def _():
        cnt_ref[...] = run_ref[...].astype(I32)


def _router(h, router, bias):
    m, d = h.shape
    ne = router.shape[1]
    tm = _pick(m, (256, 128))
    kern = functools.partial(_router_kernel, tm=tm)
    return pl.pallas_call(
        kern,
        out_shape=(jax.ShapeDtypeStruct((TOP_K, m), I32), jax.ShapeDtypeStruct((TOP_K, m), I32),
                   jax.ShapeDtypeStruct((TOP_K, m), F32), jax.ShapeDtypeStruct((ne, LANES), I32)),
        grid=(m // tm,),
        in_specs=[pl.BlockSpec((tm, d), lambda i: (i, 0)),
                  pl.BlockSpec((ne, d), lambda i: (0, 0)),
                  pl.BlockSpec((ne, 1), lambda i: (0, 0))],
        out_specs=(pl.BlockSpec((TOP_K, tm), lambda i: (0, i)),
                   pl.BlockSpec((TOP_K, tm), lambda i: (0, i)),
                   pl.BlockSpec((TOP_K, tm), lambda i: (0, i)),
                   pl.BlockSpec((ne, LANES), lambda i: (0, 0))),
        scratch_shapes=[pltpu.VMEM((ne, LANES), F32)],
        compiler_params=_cparams(("arbitrary",)),
        name="moe_router",
    )(h, router.T, bias.reshape(ne, 1))


HI_HALF = 0xFFFF0000


def _pack_halves(y):
    half = y.shape[1] // 2
    bits = lax.bitcast_convert_type(y.astype(BF16).astype(F32), jnp.uint32)
    return (bits[:, :half] >> 16) | (bits[:, half:] & jnp.uint32(HI_HALF))


def _unpack_halves(w):
    return (lax.bitcast_convert_type(w << 16, F32),
            lax.bitcast_convert_type(w & jnp.uint32(HI_HALF), F32))


def _dispatch_kernel(lo_ref, hi_ref, dest_ref, x_ref, xs_ref, packed_ref, zero_ref, sem, *, tm):
    i = pl.program_id(0)

    def zero_copy(r):
        return pltpu.make_async_copy(zero_ref.at[pl.ds(0, 1)], xs_ref.at[pl.ds(r, 1)], sem)

    @pl.when(i == 0)
    def _():
        zero_ref[...] = jnp.zeros_like(zero_ref)

        def for_pad_rows(fn):
            def span(e, c):
                def row(r, cc):
                    fn(r)
                    return cc
                return lax.fori_loop(lo_ref[e], hi_ref[e], row, c)
            lax.fori_loop(0, lo_ref.shape[0], span, 0)

        for_pad_rows(lambda r: zero_copy(r).start())
        for_pad_rows(lambda r: zero_copy(r).wait())

    packed_ref[...] = _pack_halves(x_ref[...])

    def copy(r, kk):
        return pltpu.make_async_copy(packed_ref.at[pl.ds(r, 1)], xs_ref.at[pl.ds(dest_ref[kk, r], 1)], sem)

    def issue(r, c):
        for kk in range(TOP_K):
            copy(r, kk).start()
        return c

    def drain(r, c):
        for kk in range(TOP_K):
            copy(r, kk).wait()
        return c

    lax.fori_loop(0, tm, issue, 0)
    lax.fori_loop(0, tm, drain, 0)


def _dispatch(x, dest, pad_lo, pad_hi, n_rows):
    m, d = x.shape
    tm = _pick(m, (256, 128))
    kern = functools.partial(_dispatch_kernel, tm=tm)
    return pl.pallas_call(
        kern,
        out_shape=jax.ShapeDtypeStruct((n_rows, d // 2), jnp.uint32),
        grid_spec=pltpu.PrefetchScalarGridSpec(
            num_scalar_prefetch=2,
            grid=(m // tm,),
            in_specs=[pl.BlockSpec((TOP_K, tm), lambda i, lo, hi: (0, i), memory_space=pltpu.SMEM),
                      pl.BlockSpec((tm, d), lambda i, lo, hi: (i, 0))],
            out_specs=pl.BlockSpec(memory_space=pl.ANY),
            scratch_shapes=[pltpu.VMEM((tm, d // 2), jnp.uint32), pltpu.VMEM((8, d // 2), jnp.uint32),
                            pltpu.SemaphoreType.DMA]),
        compiler_params=_cparams(("arbitrary",)),
        name="moe_dispatch",
    )(pad_lo, pad_hi, dest, x)


def _ffn_kernel(be_ref, nb_ref, x_ref, wg_ref, wu_ref, wd_ref, y_ref, wgb_ref, wub_ref, wdb_ref, *, packed):
    i = pl.program_id(0)
    prev = be_ref[jnp.maximum(i - 1, 0)]
    fresh = (i == 0) | (be_ref[i] != prev)

    @pl.when(fresh & (i < nb_ref[0]))
    def _():
        wgb_ref[...] = wg_ref[...].astype(BF16)
        wub_ref[...] = wu_ref[...].astype(BF16)
        wdb_ref[...] = wd_ref[...].astype(BF16)

    @pl.when(i >= nb_ref[0])
    def _():
        y_ref[...] = jnp.zeros_like(y_ref)

    @pl.when(i < nb_ref[0])
    def _():
        if packed:
            lo, hi = _unpack_halves(x_ref[...])
            x = jnp.concatenate([lo.astype(BF16), hi.astype(BF16)], axis=1)
        else:
            x = x_ref[...]
        a = _dot(x, wgb_ref[...])
        u = _dot(x, wub_ref[...])
        hmid = (a * jax.nn.sigmoid(a)) * u
        y = _dot(hmid.astype(BF16), wdb_ref[...])
        y_ref[...] = _pack_halves(y) if packed else y


def _ffn(xs, block_e, n_active, layer, w_gate, w_up, w_down, tm, packed):
    n_rows = xs.shape[0]
    d, ff = w_gate.shape[-2:]
    nb = n_rows // tm
    width, out_dtype = (d // 2, jnp.uint32) if packed else (d, F32)

    def xmap(i, be, nbr):
        return (jnp.maximum(jnp.minimum(i, nbr[0] - 1), 0), 0)

    def wmap(i, be, nbr):
        return (layer, be[i], 0, 0)

    return pl.pallas_call(
        functools.partial(_ffn_kernel, packed=packed),
        out_shape=jax.ShapeDtypeStruct((n_rows, width), out_dtype),
        grid_spec=pltpu.PrefetchScalarGridSpec(
            num_scalar_prefetch=2,
            grid=(nb,),
            in_specs=[pl.BlockSpec((tm, width), xmap),
                      pl.BlockSpec((None, None, d, ff), wmap),
                      pl.BlockSpec((None, None, d, ff), wmap),
                      pl.BlockSpec((None, None, ff, d), wmap)],
            out_specs=pl.BlockSpec((tm, width), lambda i, be, nbr: (i, 0)),
            scratch_shapes=[pltpu.VMEM((d, ff), BF16), pltpu.VMEM((d, ff), BF16), pltpu.VMEM((ff, d), BF16)]),
        compiler_params=_cparams(("arbitrary",)),
        name="moe_ffn",
    )(block_e, n_active, xs, w_gate, w_up, w_down)


def _combine_kernel(dest_ref, ys_ref, w_ref, sh_ref, h_ref, g_ref, b_ref, o_ref, ob_ref, buf_ref, sem, *, tm):
    def copy(r, kk):
        return pltpu.make_async_copy(ys_ref.at[pl.ds(dest_ref[kk, r], 1)],
                                     buf_ref.at[kk, pl.ds(r, 1)], sem)

    def issue(r, c):
        for kk in range(TOP_K):
            copy(r, kk).start()
        return c

    def drain(r, c):
        for kk in range(TOP_K):
            copy(r, kk).wait()
        return c

    lax.fori_loop(0, tm, issue, 0)
    lax.fori_loop(0, tm, drain, 0)
    w = w_ref[...]
    lo, hi = _unpack_halves(buf_ref[0])
    r_lo, r_hi = w[:, 0:1] * lo, w[:, 0:1] * hi
    for kk in range(1, TOP_K):
        lo, hi = _unpack_halves(buf_ref[kk])
        r_lo, r_hi = r_lo + w[:, kk:kk + 1] * lo, r_hi + w[:, kk:kk + 1] * hi
    routed = jnp.concatenate([r_lo, r_hi], axis=1)
    y = DEEPNORM_ALPHA * h_ref[...] + (routed + sh_ref[...])
    out = _layer_norm_rows(y, g_ref[...], b_ref[...])
    o_ref[...] = out
    ob_ref[...] = out.astype(BF16)


def _combine_ln(ys, dest, w_t, shared, h, g, b):
    m, d = h.shape
    tm = _pick(m, (128,))
    kern = functools.partial(_combine_kernel, tm=tm)
    return pl.pallas_call(
        kern,
        out_shape=(jax.ShapeDtypeStruct((m, d), F32), jax.ShapeDtypeStruct((m, d), BF16)),
        grid=(m // tm,),
        in_specs=[pl.BlockSpec((TOP_K, tm), lambda i: (0, i), memory_space=pltpu.SMEM),
                  pl.BlockSpec(memory_space=pl.ANY),
                  pl.BlockSpec((tm, TOP_K), lambda i: (i, 0)),
                  pl.BlockSpec((tm, d), lambda i: (i, 0)),
                  pl.BlockSpec((tm, d), lambda i: (i, 0)),
                  pl.BlockSpec((1, d), lambda i: (0, 0)),
                  pl.BlockSpec((1, d), lambda i: (0, 0))],
        out_specs=(pl.BlockSpec((tm, d), lambda i: (i, 0)),
                   pl.BlockSpec((tm, d), lambda i: (i, 0))),
        scratch_shapes=[pltpu.VMEM((TOP_K, tm, d // 2), jnp.uint32), pltpu.SemaphoreType.DMA],
        compiler_params=_cparams(("arbitrary",)),
        name="moe_combine_ln",
    )(dest, ys, w_t, shared, h, g.reshape(1, d), b.reshape(1, d))


def _moe_ln(h, h_bf, layer, router, bias, w_gate, w_up, w_down, s_gate, s_up, s_down, ln_g, ln_b):
    m, d = h.shape
    tm = _pick(m, (256, 128))
    idx, pos, wsel, counts = _router(h, router, bias)
    counts = counts[:, 0]
    padded = ((counts + tm - 1) // tm) * tm
    pend = jnp.cumsum(padded)
    pstart = pend - padded
    experts = jnp.arange(N_EXPERTS, dtype=I32)
    dest = jnp.sum(jnp.where(idx[:, :, None] == experts, pstart, 0), axis=-1) + pos
    nb = (m * TOP_K) // tm + N_EXPERTS
    block_row = jnp.arange(nb, dtype=I32)[:, None] * tm
    block_e = jnp.minimum(jnp.sum((pend[None, :] <= block_row).astype(I32), axis=1), N_EXPERTS - 1)
    n_active = (pend[-1:] // tm).astype(I32)
    tail = jnp.full((1,), nb * tm, I32)
    xs = _dispatch(h, dest, jnp.concatenate([pstart + counts, pend[-1:]]), jnp.concatenate([pend, tail]), nb * tm)
    ys = _ffn(xs, block_e, n_active, layer, w_gate, w_up, w_down, tm, packed=True)
    shared = _ffn(h_bf, jnp.zeros((m // tm,), I32), jnp.full((1,), m // tm, I32), layer,
                  s_gate[:, None], s_up[:, None], s_down[:, None], tm, packed=False)
    return _combine_ln(ys, dest, wsel.T, shared, h, ln_g, ln_b)


def kernel(x, meta, ab_w_in, s5_log_dt, s5_lambda_re, s5_lambda_im, s5_b_re, s5_b_im, s5_c_re, s5_c_im, s5_d, s5_w_glu, gla_w_gate2, gla_b_gate, gla_norm_g, ab_w_out, ret_w_in, ret_norm_g, ret_w_out, ln1_g, ln1_b, ln2_g, ln2_b, moe_router, moe_bias, moe_w_gate, moe_w_up, moe_w_down, shared_w_gate, shared_w_up, shared_w_down):
    bsz, seq, d = x.shape
    length = seq + N_META
    pad = (-length) % CHUNK
    assert pad == CHUNK - N_META, "sequence length must be a multiple of the mixer chunk"
    lp = length + pad
    m = bsz * lp
    h3 = jnp.concatenate([jnp.zeros((bsz, pad, d), x.dtype),
                          jnp.broadcast_to(meta[None].astype(x.dtype), (bsz, N_META, d)), x], axis=1)
    h = h3.reshape(m, d)
    h_bf = h.astype(BF16)
    depth = ln1_g.shape[0]
    for layer in range(depth):
        i = layer // 2
        if layer % 2 == 0:
            s5_w = s5_lambda_re.shape[1] * S5_GROUP
            gla_qk = gla_w_gate2.shape[2]
            gla_w = gla_norm_g.shape[1]
            n_main = s5_w + 2 * gla_qk + 2 * gla_w
            w_in = ab_w_in[i].astype(BF16)
            u = _matmul(h_bf, w_in, 0, s5_w, F32)
            proj = _matmul(h_bf, w_in, s5_w, n_main - s5_w, BF16)
            gate = _gla_gate(h_bf, ab_w_in[i][:, n_main:], gla_w_gate2[i], gla_b_gate[i])
            y_a = _s5_mixer(u, bsz, lp, (s5_log_dt[i], s5_lambda_re[i], s5_lambda_im[i], s5_b_re[i], s5_b_im[i],
                                         s5_c_re[i], s5_c_im[i], s5_d[i], s5_w_glu[i]))
            o = _gla(proj.reshape(bsz, lp, n_main - s5_w), gate.reshape(bsz, lp, gla_qk), gla_norm_g[i],
                     q_off=0, k_off=gla_qk, v_off=2 * gla_qk, r_off=2 * gla_qk + gla_w,
                     dk=gla_qk // GLA_HEADS, dv=gla_w // GLA_HEADS)
            mixed = jnp.concatenate([y_a, o.reshape(m, gla_w)], axis=1)
            w_out = ab_w_out[i].astype(BF16)
        else:
            ret_w = ret_norm_g.shape[1]
            ret_qk = (ret_w_in.shape[2] - 2 * ret_w) // 2
            proj = _matmul(h_bf, ret_w_in[i].astype(BF16), 0, ret_w_in.shape[2], BF16)
            o = _retention(proj.reshape(bsz, lp, ret_w_in.shape[2]), ret_norm_g[i],
                           dk=ret_qk // RET_HEADS, dv=ret_w // RET_HEADS)
            mixed = o.reshape(m, ret_w)
            w_out = ret_w_out[i].astype(BF16)
        h, h_bf = _matmul_res_ln(mixed, w_out, h, ln1_g[layer], ln1_b[layer])
        h, h_bf = _moe_ln(h, h_bf, layer, moe_router[layer], moe_bias[layer], moe_w_gate, moe_w_up, moe_w_down,
                          shared_w_gate, shared_w_up, shared_w_down, ln2_g[layer], ln2_b[layer])
    return h.reshape(bsz, lp, d)[:, pad + N_META:]
```

```python
import functools
import math

import jax
import jax.numpy as jnp
import numpy as np
from jax import lax
from jax.experimental import pallas as pl
from jax.experimental.pallas import tpu as pltpu

F32 = jnp.float32
BF16 = jnp.bfloat16
I32 = jnp.int32

N_META = 16
CHUNK = 64
LN_EPS = 1e-5
S5_GROUP = 16
S5_STATE = 64
S5_CHUNK = 16
GLA_HEADS = 4
GLA_RANK = 16
GLA_TAU = 16.0
RET_HEADS = 8
ROPE_BASE = 10000.0
N_EXPERTS = 64
TOP_K = 8
N_GROUPS = 8
TOPK_GROUPS = 4
ROUTE_SCALE = 2.5
DEPTH = 2
DEEPNORM_ALPHA = (2 * DEPTH) ** 0.25

LANES = 128
VMEM_LIMIT = 56 * 1024 * 1024


def _cparams(sem):
    return pltpu.CompilerParams(dimension_semantics=sem, vmem_limit_bytes=VMEM_LIMIT)


def _pick(n, cands):
    for c in cands:
        if n % c == 0:
            return c
    raise ValueError(f"no tile for {n} in {cands}")


def _dot(a, b):
    return jnp.dot(a, b, preferred_element_type=F32)


def _dot_nt(a, b):
    return lax.dot_general(a, b, (((1,), (1,)), ((), ())), preferred_element_type=F32)


def _dot_tn(a, b):
    return lax.dot_general(a, b, (((0,), (0,)), ((), ())), preferred_element_type=F32)


def _dot_hi(a, b):
    return jnp.dot(a, b, preferred_element_type=F32, precision=lax.Precision.HIGHEST)


def _mm_kernel(a_ref, w_ref, o_ref):
    o_ref[...] = _dot(a_ref[...], w_ref[...]).astype(o_ref.dtype)


def _matmul(a, w, col0, n_cols, out_dtype):
    m, kdim = a.shape
    tm = _pick(m, (1408, 768, 384, 128))
    tn = _pick(math.gcd(n_cols, col0) if col0 else n_cols, (512, 256, 128))
    j0 = col0 // tn
    return pl.pallas_call(
        _mm_kernel,
        out_shape=jax.ShapeDtypeStruct((m, n_cols), out_dtype),
        grid=(m // tm, n_cols // tn),
        in_specs=[pl.BlockSpec((tm, kdim), lambda i, j: (i, 0)),
                  pl.BlockSpec((kdim, tn), lambda i, j: (0, j0 + j))],
        out_specs=pl.BlockSpec((tm, tn), lambda i, j: (i, j)),
        compiler_params=_cparams(("parallel", "arbitrary")),
        name="matmul",
    )(a, w)


def _layer_norm_rows(y, g, b):
    mu = jnp.mean(y, axis=-1, keepdims=True)
    var = jnp.mean(jnp.square(y - mu), axis=-1, keepdims=True)
    return (y - mu) * lax.rsqrt(var + LN_EPS) * g + b


def _mm_ln_kernel(a_ref, w_ref, h_ref, g_ref, b_ref, o_ref, ob_ref, acc_ref):
    k = pl.program_id(1)

    @pl.when(k == 0)
    def _():
        acc_ref[...] = jnp.zeros_like(acc_ref)

    acc_ref[...] += _dot(a_ref[...], w_ref[...])

    @pl.when(k == pl.num_programs(1) - 1)
    def _():
        y = DEEPNORM_ALPHA * h_ref[...] + acc_ref[...]
        out = _layer_norm_rows(y, g_ref[...], b_ref[...])
        o_ref[...] = out
        ob_ref[...] = out.astype(BF16)


def _matmul_res_ln(a, w, h, g, b):
    m, kdim = a.shape
    d = w.shape[1]
    tm = _pick(m, (384, 128))
    tk = _pick(kdim, (512, 256, 128))
    return pl.pallas_call(
        _mm_ln_kernel,
        out_shape=(jax.ShapeDtypeStruct((m, d), F32), jax.ShapeDtypeStruct((m, d), BF16)),
        grid=(m // tm, kdim // tk),
        in_specs=[pl.BlockSpec((tm, tk), lambda i, k: (i, k)),
                  pl.BlockSpec((tk, d), lambda i, k: (k, 0)),
                  pl.BlockSpec((tm, d), lambda i, k: (i, 0)),
                  pl.BlockSpec((1, d), lambda i, k: (0, 0)),
                  pl.BlockSpec((1, d), lambda i, k: (0, 0))],
        out_specs=(pl.BlockSpec((tm, d), lambda i, k: (i, 0)),
                   pl.BlockSpec((tm, d), lambda i, k: (i, 0))),
        scratch_shapes=[pltpu.VMEM((tm, d), F32)],
        compiler_params=_cparams(("parallel", "arbitrary")),
        name="matmul_res_ln",
    )(a, w, h, g.reshape(1, d), b.reshape(1, d))


def _log_sigmoid(x):
    return jnp.minimum(x, 0.0) - jnp.log1p(jnp.exp(-jnp.abs(x)))


def _gate_kernel(h_ref, wl_ref, w2_ref, b_ref, o_ref):
    low = _dot(h_ref[...], wl_ref[...])
    pre = _dot(low.astype(BF16), w2_ref[...]) + b_ref[...]
    o_ref[...] = _log_sigmoid(pre) / GLA_TAU


def _gla_gate(h_bf, w_low, w_gate2, b_gate):
    m, d = h_bf.shape
    qk = w_gate2.shape[1]
    tm = _pick(m, (256, 128))
    wl = jnp.zeros((d, LANES), BF16).at[:, :GLA_RANK].set(w_low.astype(BF16))
    w2 = jnp.zeros((LANES, qk), BF16).at[:GLA_RANK].set(w_gate2.astype(BF16))
    return pl.pallas_call(
        _gate_kernel,
        out_shape=jax.ShapeDtypeStruct((m, qk), F32),
        grid=(m // tm,),
        in_specs=[pl.BlockSpec((tm, d), lambda i: (i, 0)),
                  pl.BlockSpec((d, LANES), lambda i: (0, 0)),
                  pl.BlockSpec((LANES, qk), lambda i: (0, 0)),
                  pl.BlockSpec((1, qk), lambda i: (0, 0))],
        out_specs=pl.BlockSpec((tm, qk), lambda i: (i, 0)),
        compiler_params=_cparams(("parallel",)),
        name="gla_gate",
    )(h_bf, wl, w2, b_gate.reshape(1, qk))


def _head_norm_rows(o, gain):
    mu = jnp.mean(o, axis=-1, keepdims=True)
    var = jnp.mean(jnp.square(o - mu), axis=-1, keepdims=True)
    return (o - mu) * lax.rsqrt(var + LN_EPS) * gain


HEADS_PER_STEP = 2


def _gla_kernel(q_ref, k_ref, v_ref, r_ref, g_ref, gain_ref, o_ref, s_ref, *, n_chunks, n_pad, dk, dv):
    s_ref[...] = jnp.zeros_like(s_ref)
    row = lax.broadcasted_iota(I32, (CHUNK, CHUNK), 0)
    col = lax.broadcasted_iota(I32, (CHUNK, CHUNK), 1)
    tril = jnp.where(row >= col, 1.0, 0.0).astype(F32)
    first_valid = jnp.where(lax.broadcasted_iota(I32, (CHUNK, 1), 0) >= n_pad, 1.0, 0.0)
    scale = dk ** -0.5

    def chunk(n, carry):
        sl = pl.ds(pl.multiple_of(n * CHUNK, CHUNK), CHUNK)
        valid = jnp.where(n == 0, first_valid, jnp.ones_like(first_valid))
        for hh in range(HEADS_PER_STEP):
            ck = slice(hh * dk, (hh + 1) * dk)
            cv = slice(hh * dv, (hh + 1) * dv)
            q = q_ref[sl, ck].astype(F32) * scale * valid
            k = k_ref[sl, ck].astype(F32) * valid
            vb = (v_ref[sl, cv].astype(F32) * valid).astype(BF16)
            g = g_ref[sl, ck] * valid
            b = _dot_hi(tril, g)
            b_last = b[CHUNK - 1:CHUNK, :]
            q_in = (q * jnp.exp(b)).astype(BF16)
            k_out = k * jnp.exp(-b)
            k_end = k * jnp.exp(b_last - b)
            scores = _dot_nt(q_in, k_out.astype(BF16)) * tril
            o = _dot(scores.astype(BF16), vb) + _dot_nt(q_in, s_ref[hh].astype(BF16))
            s_ref[hh] = s_ref[hh] * jnp.exp(b_last) + _dot_tn(vb, k_end.astype(BF16))
            r = r_ref[sl, cv].astype(F32)
            o_ref[sl, cv] = (_head_norm_rows(o, gain_ref[:, cv]) * (r * jax.nn.sigmoid(r))).astype(o_ref.dtype)
        return carry

    lax.fori_loop(0, n_chunks, chunk, 0)


def _gla(proj3, gate3, norm_g, *, q_off, k_off, v_off, r_off, dk, dv):
    bsz, lp, _ = proj3.shape
    heads = GLA_HEADS
    hp = HEADS_PER_STEP
    wk, wv = hp * dk, hp * dv
    kern = functools.partial(_gla_kernel, n_chunks=lp // CHUNK, n_pad=CHUNK - N_META, dk=dk, dv=dv)
    return pl.pallas_call(
        kern,
        out_shape=jax.ShapeDtypeStruct((bsz, lp, heads * dv), BF16),
        grid=(bsz, heads // hp),
        in_specs=[pl.BlockSpec((None, lp, wk), lambda b, h: (b, 0, q_off // wk + h)),
                  pl.BlockSpec((None, lp, wk), lambda b, h: (b, 0, k_off // wk + h)),
                  pl.BlockSpec((None, lp, wv), lambda b, h: (b, 0, v_off // wv + h)),
                  pl.BlockSpec((None, lp, wv), lambda b, h: (b, 0, r_off // wv + h)),
                  pl.BlockSpec((None, lp, wk), lambda b, h: (b, 0, h)),
                  pl.BlockSpec((1, wv), lambda b, h: (0, h))],
        out_specs=pl.BlockSpec((None, lp, wv), lambda b, h: (b, 0, h)),
        scratch_shapes=[pltpu.VMEM((hp, dv, dk), F32)],
        compiler_params=_cparams(("parallel", "parallel")),
        name="gla",
    )(proj3, proj3, proj3, proj3, gate3, norm_g.reshape(1, heads * dv))


def _ret_kernel(q_ref, k_ref, v_ref, gt_ref, cos_ref, sin_ref, dm_ref, xi_ref, zeta_ref, cd_ref, gain_ref,
                o_ref, s_ref, *, n_chunks, n_pad, dk, dv):
    s_ref[...] = jnp.zeros_like(s_ref)
    first_valid = jnp.where(lax.broadcasted_iota(I32, (CHUNK, 1), 0) >= n_pad, 1.0, 0.0)
    scale = dk ** -0.5
    half = dk // 2

    def rot(t, cos, sin):
        t1, t2 = t[:, :half], t[:, half:]
        return jnp.concatenate([t1 * cos - t2 * sin, t1 * sin + t2 * cos], axis=-1)

    def chunk(n, carry):
        sl = pl.ds(pl.multiple_of(n * CHUNK, CHUNK), CHUNK)
        valid = jnp.where(n == 0, first_valid, jnp.ones_like(first_valid))
        cos, sin = cos_ref[sl, :], sin_ref[sl, :]
        for hh in range(HEADS_PER_STEP):
            ck = slice(hh * dk, (hh + 1) * dk)
            cv = slice(hh * dv, (hh + 1) * dv)
            q = rot(q_ref[sl, ck].astype(F32), cos, sin) * valid
            k = rot(k_ref[sl, ck].astype(F32), cos, sin) * (scale * valid)
            vb = (v_ref[sl, cv].astype(F32) * valid).astype(BF16)
            scores = _dot_nt(q.astype(BF16), k.astype(BF16)) * dm_ref[hh]
            o = _dot(scores.astype(BF16), vb) + _dot((q * xi_ref[hh]).astype(BF16), s_ref[hh].astype(BF16))
            s_ref[hh] = s_ref[hh] * cd_ref[hh] + _dot_tn((k * zeta_ref[hh]).astype(BF16), vb)
            gt = gt_ref[sl, cv].astype(F32)
            o_ref[sl, cv] = (_head_norm_rows(o, gain_ref[:, cv]) * (gt * jax.nn.sigmoid(gt))).astype(o_ref.dtype)
        return carry

    lax.fori_loop(0, n_chunks, chunk, 0)


def _retention(proj3, norm_g, *, dk, dv):
    bsz, lp, _ = proj3.shape
    heads = RET_HEADS
    half = dk // 2
    pos_tok = jnp.arange(lp, dtype=F32) - float(CHUNK - N_META)
    inv = ROPE_BASE ** (-jnp.arange(0, dk, 2, dtype=F32) / dk)
    ang = pos_tok[:, None] * inv[None, :]
    cos_t, sin_t = jnp.cos(ang), jnp.sin(ang)
    log_gamma = jnp.log(1.0 - 2.0 ** (-5.0 - jnp.arange(heads, dtype=F32)))
    pos = jnp.arange(CHUNK, dtype=F32)
    diff = pos[:, None] - pos[None, :]
    dmask = jnp.where(diff >= 0, jnp.exp(log_gamma[:, None, None] * jnp.maximum(diff, 0.0)), 0.0)
    xi = jnp.exp(log_gamma[:, None] * (pos + 1.0))[:, :, None]
    zeta = jnp.exp(log_gamma[:, None] * (CHUNK - 1.0 - pos))[:, :, None]
    cdec = jnp.exp(log_gamma * CHUNK)[:, None, None]
    kern = functools.partial(_ret_kernel, n_chunks=lp // CHUNK, n_pad=CHUNK - N_META, dk=dk, dv=dv)
    hp = HEADS_PER_STEP
    wk, wv = hp * dk, hp * dv
    ng = heads // hp
    nv = (2 * heads * dk) // wv
    return pl.pallas_call(
        kern,
        out_shape=jax.ShapeDtypeStruct((bsz, lp, heads * dv), BF16),
        grid=(bsz, ng),
        in_specs=[pl.BlockSpec((None, lp, wk), lambda b, h: (b, 0, h)),
                  pl.BlockSpec((None, lp, wk), lambda b, h: (b, 0, ng + h)),
                  pl.BlockSpec((None, lp, wv), lambda b, h: (b, 0, nv + h)),
                  pl.BlockSpec((None, lp, wv), lambda b, h: (b, 0, nv + ng + h)),
                  pl.BlockSpec((lp, half), lambda b, h: (0, 0)),
                  pl.BlockSpec((lp, half), lambda b, h: (0, 0)),
                  pl.BlockSpec((hp, CHUNK, CHUNK), lambda b, h: (h, 0, 0)),
                  pl.BlockSpec((hp, CHUNK, 1), lambda b, h: (h, 0, 0)),
                  pl.BlockSpec((hp, CHUNK, 1), lambda b, h: (h, 0, 0)),
                  pl.BlockSpec((hp, 1, 1), lambda b, h: (h, 0, 0)),
                  pl.BlockSpec((1, wv), lambda b, h: (0, h))],
        out_specs=pl.BlockSpec((None, lp, wv), lambda b, h: (b, 0, h)),
        scratch_shapes=[pltpu.VMEM((hp, dk, dv), F32)],
        compiler_params=_cparams(("parallel", "parallel")),
        name="retention",
    )(proj3, proj3, proj3, proj3, cos_t, sin_t, dmask, xi, zeta, cdec, norm_g.reshape(1, heads * dv))


def _s5_prepare(log_dt, lam_re, lam_im, b_re, b_im, c_re, c_im, n_scan):
    cs = S5_CHUNK
    groups, states = lam_re.shape
    dt = jnp.exp(log_dt.astype(F32))[:, None]
    lr, li = lam_re.astype(F32), lam_im.astype(F32)
    mag = jnp.exp(lr * dt)
    ab_re, ab_im = mag * jnp.cos(li * dt), mag * jnp.sin(li * dt)
    den = lr * lr + li * li
    nr, ni = ab_re - 1.0, ab_im
    z_re = (nr * lr + ni * li) / den
    z_im = (ni * lr - nr * li) / den
    bz_re = z_re[..., None] * b_re - z_im[..., None] * b_im
    bz_im = z_re[..., None] * b_im + z_im[..., None] * b_re

    def power(kk):
        kk = kk.astype(F32)[..., None, None]
        m = jnp.exp(kk * (lr * dt))
        return m * jnp.cos(kk * (li * dt)), m * jnp.sin(kk * (li * dt))

    pw_re, pw_im = power(jnp.arange(cs + 1))
    cp_re = c_re[None] * pw_re[:, :, None, :] - c_im[None] * pw_im[:, :, None, :]
    cp_im = c_re[None] * pw_im[:, :, None, :] + c_im[None] * pw_re[:, :, None, :]
    hi = lax.Precision.HIGHEST
    kern = (jnp.einsum('tgcp,gpd->tgcd', cp_re, bz_re, precision=hi)
            - jnp.einsum('tgcp,gpd->tgcd', cp_im, bz_im, precision=hi))
    rev = pw_re[cs - 1 - jnp.arange(cs)], pw_im[cs - 1 - jnp.arange(cs)]
    win_re = rev[0][..., None] * bz_re[None] - rev[1][..., None] * bz_im[None]
    win_im = rev[0][..., None] * bz_im[None] + rev[1][..., None] * bz_re[None]
    win = jnp.concatenate([win_re, win_im], axis=2)
    win = win.transpose(1, 0, 3, 2).reshape(groups, cs * S5_GROUP, 2 * states)
    wout = jnp.concatenate([cp_re[1:], -cp_im[1:]], axis=3)
    wout = wout.transpose(1, 3, 0, 2).reshape(groups, 2 * states, cs * S5_GROUP)
    sc_re, sc_im = power(cs * (2 ** jnp.arange(n_scan)))
    mul_r = jnp.concatenate([sc_re, sc_re], axis=-1)
    mul_i = jnp.concatenate([-sc_im, sc_im], axis=-1)
    scan_mul = jnp.stack([mul_r, mul_i], axis=2).transpose(1, 0, 2, 3)
    return kern[:cs].astype(BF16), win.astype(BF16), wout.astype(BF16), scan_mul


def _s5_octets(kern, win, wout, scan_mul):
    cs, c = S5_CHUNK, S5_GROUP
    groups = kern.shape[1]
    go = LANES // c
    no = groups // go
    st = win.shape[-1]
    n_scan = scan_mul.shape[1]
    lane = jnp.arange(LANES)
    spread = (lane[None, :] % c == jnp.arange(c)[:, None]).astype(BF16)
    kt = kern.reshape(cs, no, go, c, c).transpose(1, 0, 2, 4, 3).reshape(no, cs, LANES, c)
    lag_blocks = jnp.where(lane[:, None] // c == lane[None, :] // c,
                           jnp.einsum('olic,cj->olij', kt, spread, preferred_element_type=F32), 0.0)
    rows_in = jnp.arange(cs * LANES)
    w2 = win.reshape(no, go, cs, c, st).transpose(0, 2, 1, 3, 4).reshape(no, cs * LANES, st)
    win_o = jnp.where(((rows_in % LANES) // c)[:, None] == jnp.arange(go * st)[None, :] // st,
                      jnp.tile(w2, (1, 1, go)), 0.0)
    cols_out = jnp.arange(cs * LANES)
    src = jnp.arange(cs * c)
    spread_t = ((src[:, None] // c == cols_out[None, :] // LANES)
                & (src[:, None] % c == cols_out[None, :] % c)).astype(BF16)
    wout_o = jnp.where(jnp.arange(go * st)[:, None] // st == ((cols_out % LANES) // c)[None, :],
                       jnp.einsum('oik,kj->oij', wout.reshape(no, go * st, cs * c), spread_t,
                                  preferred_element_type=F32), 0.0)
    mul_o = scan_mul.reshape(no, go, n_scan, 2, st).transpose(0, 2, 3, 1, 4).reshape(no, n_scan, 2, go * st)
    return lag_blocks.astype(BF16), win_o.astype(BF16), wout_o.astype(BF16), mul_o


def _s5_kernel(u_ref, lag_ref, win_ref, wout_ref, mul_ref, y_ref, x_ref, start_ref, toep_ref, *,
               bsz, n_steps, n_scan, n_pad_steps):
    j = pl.program_id(1)
    rows = bsz * n_steps
    cs = S5_CHUNK
    t_half = y_ref.shape[1]

    for jv in range(cs // t_half):
        @pl.when(j == jv)
        def _(jv=jv):
            for s in range(cs):
                for tt in range(t_half):
                    lag = jv * t_half + tt - s
                    blk = lag_ref[lag] if lag >= 0 else jnp.zeros((LANES, LANES), BF16)
                    toep_ref[s * LANES:(s + 1) * LANES, tt * LANES:(tt + 1) * LANES] = blk

    @pl.when(j == 0)
    def _():
        step = lax.broadcasted_iota(I32, (rows, 1), 0)
        for bb in range(1, bsz):
            step = step - jnp.where(lax.broadcasted_iota(I32, (rows, 1), 0) >= bb * n_steps, n_steps, 0)
        keep = step >= n_pad_steps
        for s in range(cs):
            x_ref[:, s * LANES:(s + 1) * LANES] = jnp.where(keep, u_ref[:, s, :], 0.0).astype(BF16)
        acc = _dot(x_ref[...], win_ref[...])
        k_st = acc.shape[1]
        re_lane = (lax.broadcasted_iota(I32, (1, k_st), 1) & (LANES - 1)) < LANES // 2
        for jj in range(n_scan):
            sh = 2 ** jj
            prev = jnp.where(step >= sh, pltpu.roll(acc, sh, 0), 0.0)
            swapped = jnp.where(re_lane, pltpu.roll(prev, k_st - LANES // 2, 1), pltpu.roll(prev, LANES // 2, 1))
            acc = acc + mul_ref[jj, 0:1, :] * prev + mul_ref[jj, 1:2, :] * swapped
        start = jnp.where(step >= 1, pltpu.roll(acc, 1, 0), 0.0)
        start_ref[...] = start.astype(BF16)

    y = _dot(x_ref[...], toep_ref[...]) + _dot(start_ref[...], wout_ref[...])
    for tt in range(t_half):
        y_ref[:, tt, :] = y[:, tt * LANES:(tt + 1) * LANES]


def _s5_scan(u3, lag_blocks, win, wout, scan_mul, *, bsz, n_steps, n_pad_steps):
    rows, cs, width = u3.shape
    no, k_in, k_st = win.shape
    n_scan = scan_mul.shape[1]
    t_half = cs // 2
    kern = functools.partial(_s5_kernel, bsz=bsz, n_steps=n_steps, n_scan=n_scan, n_pad_steps=n_pad_steps)
    return pl.pallas_call(
        kern,
        out_shape=jax.ShapeDtypeStruct((rows, cs, width), F32),
        grid=(no, cs // t_half),
        in_specs=[pl.BlockSpec((rows, cs, LANES), lambda o, j: (0, 0, o)),
                  pl.BlockSpec((None, cs, LANES, LANES), lambda o, j: (o, 0, 0, 0)),
                  pl.BlockSpec((None, k_in, k_st), lambda o, j: (o, 0, 0)),
                  pl.BlockSpec((None, k_st, t_half * LANES), lambda o, j: (o, 0, j)),
                  pl.BlockSpec((None, n_scan, 2, k_st), lambda o, j: (o, 0, 0, 0))],
        out_specs=pl.BlockSpec((rows, t_half, LANES), lambda o, j: (0, j, o)),
        scratch_shapes=[pltpu.VMEM((rows, k_in), BF16), pltpu.VMEM((rows, k_st), BF16),
                        pltpu.VMEM((k_in, t_half * LANES), BF16)],
        compiler_params=_cparams(("parallel", "arbitrary")),
        name="s5_scan",
    )(u3, lag_blocks, win, wout, scan_mul)


def _glu_kernel(y_ref, u_ref, d_ref, w_ref, o_ref):
    z = jax.nn.gelu(y_ref[...] + d_ref[...] * u_ref[...])
    o_ref[...] = (z * jax.nn.sigmoid(_dot(z.astype(BF16), w_ref[...]))).astype(o_ref.dtype)


def _s5_glu(y, proj, d, w_glu_bf):
    m, width = y.shape
    tm = _pick(m, (256, 128))
    return pl.pallas_call(
        _glu_kernel,
        out_shape=jax.ShapeDtypeStruct((m, width), BF16),
        grid=(m // tm,),
        in_specs=[pl.BlockSpec((tm, width), lambda i: (i, 0)),
                  pl.BlockSpec((tm, width), lambda i: (i, 0)),
                  pl.BlockSpec((1, width), lambda i: (0, 0)),
                  pl.BlockSpec((width, width), lambda i: (0, 0))],
        out_specs=pl.BlockSpec((tm, width), lambda i: (i, 0)),
        compiler_params=_cparams(("parallel",)),
        name="s5_glu",
    )(y, proj, d.reshape(1, width), w_glu_bf)


def _s5_mixer(proj, bsz, lp, params):
    log_dt, lam_re, lam_im, b_re, b_im, c_re, c_im, d, w_glu = params
    groups = lam_re.shape[0]
    width = groups * S5_GROUP
    cs = S5_CHUNK
    n_steps = lp // cs
    assert (bsz * n_steps) % 8 == 0, "super-step rows must fill whole sublane tiles"
    n_scan = max(1, math.ceil(math.log2(n_steps)))
    mats = _s5_octets(*_s5_prepare(log_dt, lam_re, lam_im, b_re, b_im, c_re, c_im, n_scan))
    y = _s5_scan(proj[:, :width].reshape(bsz * n_steps, cs, width), *mats, bsz=bsz, n_steps=n_steps,
                 n_pad_steps=(CHUNK - N_META) // cs)
    return _s5_glu(y.reshape(bsz * lp, width), proj, d, w_glu.astype(BF16))


def _router_kernel(x_ref, rt_ref, bias_ref, idx_ref, pos_ref, w_ref, cnt_ref, run_ref, *, tm):
    i = pl.program_id(0)

    @pl.when(i == 0)
    def _():
        run_ref[...] = jnp.zeros_like(run_ref)

    ne, ng = N_EXPERTS, N_GROUPS
    per = ne // ng
    neg = -jnp.inf
    logits = lax.dot_general(rt_ref[...], x_ref[...], (((1,), (1,)), ((), ())),
                             preferred_element_type=F32, precision=lax.Precision.HIGHEST)
    s = jax.nn.sigmoid(logits)
    sb = s + bias_ref[...]
    sb3 = sb.reshape(ng, per, tm)
    io = lax.broadcasted_iota(I32, (ng, per, tm), 1)
    m1 = jnp.max(sb3, axis=1, keepdims=True)
    first = jnp.min(jnp.where(sb3 == m1, io, per), axis=1, keepdims=True)
    m2 = jnp.max(jnp.where(io == first, neg, sb3), axis=1, keepdims=True)
    gs = m1 + m2
    gi = lax.broadcasted_iota(I32, (ng, 1, tm), 0)
    grank = jnp.zeros((ng, 1, tm), F32)
    for g2 in range(ng):
        other = gs[g2:g2 + 1]
        beats = (other > gs) | ((other == gs) & (g2 < gi))
        grank = grank + jnp.where(beats, 1.0, 0.0)
    gsel = grank < float(TOPK_GROUPS)
    ms = jnp.where(gsel, sb3, neg).reshape(ne, tm)
    ei = lax.broadcasted_iota(I32, (ne, tm), 0)
    rank = jnp.zeros((ne, tm), F32)
    for e2 in range(ne):
        other = ms[e2:e2 + 1, :]
        beats = (other > ms) | ((other == ms) & (e2 < ei))
        rank = rank + jnp.where(beats, 1.0, 0.0)
    sel = rank < float(TOP_K)
    self_ = jnp.where(sel, 1.0, 0.0)
    wsum = jnp.sum(jnp.where(sel, s, 0.0), axis=0, keepdims=True)
    wgt = s / wsum * ROUTE_SCALE
    ti = lax.broadcasted_iota(I32, (tm, tm), 0)
    tj = lax.broadcasted_iota(I32, (tm, tm), 1)
    upper = jnp.where(ti < tj, 1.0, 0.0).astype(BF16)
    pos = _dot(self_.astype(BF16), upper) + run_ref[:, 0:1]
    run_ref[...] = run_ref[...] + jnp.sum(self_, axis=1, keepdims=True)
    li = lax.broadcasted_iota(I32, (ne, ne), 0)
    lj = lax.broadcasted_iota(I32, (ne, ne), 1)
    lower = jnp.where(lj < li, 1.0, 0.0).astype(BF16)
    slot = _dot(lower, self_.astype(BF16))
    eif = ei.astype(F32)
    idx_rows, pos_rows, w_rows = [], [], []
    for kk in range(TOP_K):
        one = sel & (slot == float(kk))
        idx_rows.append(jnp.sum(jnp.where(one, eif, 0.0), axis=0, keepdims=True))
        pos_rows.append(jnp.sum(jnp.where(one, pos, 0.0), axis=0, keepdims=True))
        w_rows.append(jnp.sum(jnp.where(one, wgt, 0.0), axis=0, keepdims=True))
    idx_ref[...] = jnp.concatenate(idx_rows, axis=0).astype(I32)
    pos_ref[...] = jnp.concatenate(pos_rows, axis=0).astype(I32)
    w_ref[...] = jnp.concatenate(w_rows, axis=0)

    @pl.when(i == pl.num_programs(0) - 1)
    def _():
        cnt_ref[...] = run_ref[...].astype(I32)


def _router(h, router, bias):
    m, d = h.shape
    ne = router.shape[1]
    tm = _pick(m, (256, 128))
    kern = functools.partial(_router_kernel, tm=tm)
    return pl.pallas_call(
        kern,
        out_shape=(jax.ShapeDtypeStruct((TOP_K, m), I32), jax.ShapeDtypeStruct((TOP_K, m), I32),
                   jax.ShapeDtypeStruct((TOP_K, m), F32), jax.ShapeDtypeStruct((ne, LANES), I32)),
        grid=(m // tm,),
        in_specs=[pl.BlockSpec((tm, d), lambda i: (i, 0)),
                  pl.BlockSpec((ne, d), lambda i: (0, 0)),
                  pl.BlockSpec((ne, 1), lambda i: (0, 0))],
        out_specs=(pl.BlockSpec((TOP_K, tm), lambda i: (0, i)),
                   pl.BlockSpec((TOP_K, tm), lambda i: (0, i)),
                   pl.BlockSpec((TOP_K, tm), lambda i: (0, i)),
                   pl.BlockSpec((ne, LANES), lambda i: (0, 0))),
        scratch_shapes=[pltpu.VMEM((ne, LANES), F32)],
        compiler_params=_cparams(("arbitrary",)),
        name="moe_router",
    )(h, router.T, bias.reshape(ne, 1))


HI_HALF = 0xFFFF0000


def _pack_halves(y):
    half = y.shape[1] // 2
    bits = lax.bitcast_convert_type(y.astype(BF16).astype(F32), jnp.uint32)
    return (bits[:, :half] >> 16) | (bits[:, half:] & jnp.uint32(HI_HALF))


def _unpack_halves(w):
    return (lax.bitcast_convert_type(w << 16, F32),
            lax.bitcast_convert_type(w & jnp.uint32(HI_HALF), F32))


def _dispatch_kernel(lo_ref, hi_ref, dest_ref, x_ref, xs_ref, packed_ref, zero_ref, sem, *, tm):
    i = pl.program_id(0)

    def zero_copy(r):
        return pltpu.make_async_copy(zero_ref.at[pl.ds(0, 1)], xs_ref.at[pl.ds(r, 1)], sem)

    @pl.when(i == 0)
    def _():
        zero_ref[...] = jnp.zeros_like(zero_ref)

        def for_pad_rows(fn):
            def span(e, c):
                def row(r, cc):
                    fn(r)
                    return cc
                return lax.fori_loop(lo_ref[e], hi_ref[e], row, c)
            lax.fori_loop(0, lo_ref.shape[0], span, 0)

        for_pad_rows(lambda r: zero_copy(r).start())
        for_pad_rows(lambda r: zero_copy(r).wait())

    packed_ref[...] = _pack_halves(x_ref[...])

    def copy(r, kk):
        return pltpu.make_async_copy(packed_ref.at[pl.ds(r, 1)], xs_ref.at[pl.ds(dest_ref[kk, r], 1)], sem)

    def issue(r, c):
        for kk in range(TOP_K):
            copy(r, kk).start(priority=kk % 2)
        return c

    def drain(r, c):
        for kk in range(TOP_K):
            copy(r, kk).wait()
        return c

    lax.fori_loop(0, tm, issue, 0)
    lax.fori_loop(0, tm, drain, 0)


def _dispatch(x, dest, pad_lo, pad_hi, n_rows):
    m, d = x.shape
    tm = _pick(m, (256, 128))
    kern = functools.partial(_dispatch_kernel, tm=tm)
    return pl.pallas_call(
        kern,
        out_shape=jax.ShapeDtypeStruct((n_rows, d // 2), jnp.uint32),
        grid_spec=pltpu.PrefetchScalarGridSpec(
            num_scalar_prefetch=2,
            grid=(m // tm,),
            in_specs=[pl.BlockSpec((TOP_K, tm), lambda i, lo, hi: (0, i), memory_space=pltpu.SMEM),
                      pl.BlockSpec((tm, d), lambda i, lo, hi: (i, 0))],
            out_specs=pl.BlockSpec(memory_space=pl.ANY),
            scratch_shapes=[pltpu.VMEM((tm, d // 2), jnp.uint32), pltpu.VMEM((8, d // 2), jnp.uint32),
                            pltpu.SemaphoreType.DMA]),
        compiler_params=_cparams(("arbitrary",)),
        name="moe_dispatch",
    )(pad_lo, pad_hi, dest, x)


def _ffn_kernel(be_ref, nb_ref, nxt_ref, slot_ref, x_ref, wg_hbm, wu_hbm, wd_hbm, y_ref,
                wg_buf, wu_buf, wd_buf, wgb_ref, wub_ref, wdb_ref, sem, *, packed, layer):
    i = pl.program_id(0)
    prev = be_ref[jnp.maximum(i - 1, 0)]
    fresh = (i == 0) | (be_ref[i] != prev)

    def fetch(e, s):
        return [pltpu.make_async_copy(hbm.at[layer, e], buf.at[s], sem.at[s, n])
                for n, (hbm, buf) in enumerate(((wg_hbm, wg_buf), (wu_hbm, wu_buf), (wd_hbm, wd_buf)))]

    @pl.when(i == 0)
    def _():
        for c in fetch(be_ref[0], slot_ref[0]):
            c.start()

    @pl.when(fresh & (i < nb_ref[0]))
    def _():
        s = slot_ref[i]
        for c in fetch(be_ref[i], s):
            c.wait()

        @pl.when(nxt_ref[i] >= 0)
        def _():
            for c in fetch(nxt_ref[i], 1 - s):
                c.start()

        wgb_ref[...] = wg_buf[s].astype(BF16)
        wub_ref[...] = wu_buf[s].astype(BF16)
        wdb_ref[...] = wd_buf[s].astype(BF16)

    @pl.when(i >= nb_ref[0])
    def _():
        y_ref[...] = jnp.zeros_like(y_ref)

    @pl.when(i < nb_ref[0])
    def _():
        if packed:
            lo, hi = _unpack_halves(x_ref[...])
            x = jnp.concatenate([lo.astype(BF16), hi.astype(BF16)], axis=1)
        else:
            x = x_ref[...]
        a = _dot(x, wgb_ref[...])
        u = _dot(x, wub_ref[...])
        hmid = (a * jax.nn.sigmoid(a)) * u
        y = _dot(hmid.astype(BF16), wdb_ref[...])
        y_ref[...] = _pack_halves(y) if packed else y


def _ffn(xs, block_e, n_active, next_e, slot, layer, w_gate, w_up, w_down, tm, packed):
    n_rows = xs.shape[0]
    d, ff = w_gate.shape[-2:]
    nb = n_rows // tm
    width, out_dtype = (d // 2, jnp.uint32) if packed else (d, F32)

    def xmap(i, be, nbr, nxt, sl):
        return (jnp.maximum(jnp.minimum(i, nbr[0] - 1), 0), 0)

    return pl.pallas_call(
        functools.partial(_ffn_kernel, packed=packed, layer=layer),
        out_shape=jax.ShapeDtypeStruct((n_rows, width), out_dtype),
        grid_spec=pltpu.PrefetchScalarGridSpec(
            num_scalar_prefetch=4,
            grid=(nb,),
            in_specs=[pl.BlockSpec((tm, width), xmap),
                      pl.BlockSpec(memory_space=pl.ANY),
                      pl.BlockSpec(memory_space=pl.ANY),
                      pl.BlockSpec(memory_space=pl.ANY)],
            out_specs=pl.BlockSpec((tm, width), lambda i, be, nbr, nxt, sl: (i, 0)),
            scratch_shapes=[pltpu.VMEM((2, d, ff), F32), pltpu.VMEM((2, d, ff), F32), pltpu.VMEM((2, ff, d), F32),
                            pltpu.VMEM((d, ff), BF16), pltpu.VMEM((d, ff), BF16), pltpu.VMEM((ff, d), BF16),
                            pltpu.SemaphoreType.DMA((2, 3))]),
        compiler_params=_cparams(("arbitrary",)),
        name="moe_ffn",
    )(block_e, n_active, next_e, slot, xs, w_gate, w_up, w_down)


def _combine_kernel(dest_ref, ys_ref, w_ref, sh_ref, h_ref, g_ref, b_ref, o_ref, ob_ref, buf_ref, sem, *, tm):
    def copy(r, kk):
        return pltpu.make_async_copy(ys_ref.at[pl.ds(dest_ref[kk, r], 1)],
                                     buf_ref.at[kk, pl.ds(r, 1)], sem)

    def issue(r, c):
        for kk in range(TOP_K):
            copy(r, kk).start(priority=kk % 2)
        return c

    def drain(r, c):
        for kk in range(TOP_K):
            copy(r, kk).wait()
        return c

    lax.fori_loop(0, tm, issue, 0)
    lax.fori_loop(0, tm, drain, 0)
    w = w_ref[...]
    lo, hi = _unpack_halves(buf_ref[0])
    r_lo, r_hi = w[:, 0:1] * lo, w[:, 0:1] * hi
    for kk in range(1, TOP_K):
        lo, hi = _unpack_halves(buf_ref[kk])
        r_lo, r_hi = r_lo + w[:, kk:kk + 1] * lo, r_hi + w[:, kk:kk + 1] * hi
    routed = jnp.concatenate([r_lo, r_hi], axis=1)
    y = DEEPNORM_ALPHA * h_ref[...] + (routed + sh_ref[...])
    out = _layer_norm_rows(y, g_ref[...], b_ref[...])
    o_ref[...] = out
    ob_ref[...] = out.astype(BF16)


def _combine_ln(ys, dest, w_t, shared, h, g, b):
    m, d = h.shape
    tm = _pick(m, (128,))
    kern = functools.partial(_combine_kernel, tm=tm)
    return pl.pallas_call(
        kern,
        out_shape=(jax.ShapeDtypeStruct((m, d), F32), jax.ShapeDtypeStruct((m, d), BF16)),
        grid=(m // tm,),
        in_specs=[pl.BlockSpec((TOP_K, tm), lambda i: (0, i), memory_space=pltpu.SMEM),
                  pl.BlockSpec(memory_space=pl.ANY),
                  pl.BlockSpec((tm, TOP_K), lambda i: (i, 0)),
                  pl.BlockSpec((tm, d), lambda i: (i, 0)),
                  pl.BlockSpec((tm, d), lambda i: (i, 0)),
                  pl.BlockSpec((1, d), lambda i: (0, 0)),
                  pl.BlockSpec((1, d), lambda i: (0, 0))],
        out_specs=(pl.BlockSpec((tm, d), lambda i: (i, 0)),
                   pl.BlockSpec((tm, d), lambda i: (i, 0))),
        scratch_shapes=[pltpu.VMEM((TOP_K, tm, d // 2), jnp.uint32), pltpu.SemaphoreType.DMA],
        compiler_params=_cparams(("arbitrary",)),
        name="moe_combine_ln",
    )(dest, ys, w_t, shared, h, g.reshape(1, d), b.reshape(1, d))


def _moe_ln(h, h_bf, layer, router, bias, w_gate, w_up, w_down, s_gate, s_up, s_down, ln_g, ln_b):
    m, d = h.shape
    tm = _pick(m, (256, 128))
    idx, pos, wsel, counts = _router(h, router, bias)
    counts = counts[:, 0]
    padded = ((counts + tm - 1) // tm) * tm
    pend = jnp.cumsum(padded)
    pstart = pend - padded
    experts = jnp.arange(N_EXPERTS, dtype=I32)
    dest = jnp.sum(jnp.where(idx[:, :, None] == experts, pstart, 0), axis=-1) + pos
    nb = (m * TOP_K) // tm + N_EXPERTS
    block_row = jnp.arange(nb, dtype=I32)[:, None] * tm
    block_e = jnp.minimum(jnp.sum((pend[None, :] <= block_row).astype(I32), axis=1), N_EXPERTS - 1)
    n_active = (pend[-1:] // tm).astype(I32)
    tail = jnp.full((1,), nb * tm, I32)
    xs = _dispatch(h, dest, jnp.concatenate([pstart + counts, pend[-1:]]), jnp.concatenate([pend, tail]), nb * tm)
    end_block = (pend // tm)[block_e]
    next_e = jnp.where(end_block < n_active[0], block_e[jnp.minimum(end_block, nb - 1)], -1).astype(I32)
    slot = ((jnp.cumsum((counts > 0).astype(I32)) - 1)[block_e] % 2).astype(I32)
    ys = _ffn(xs, block_e, n_active, next_e, slot, layer, w_gate, w_up, w_down, tm, packed=True)
    one = jnp.zeros((m // tm,), I32)
    shared = _ffn(h_bf, one, jnp.full((1,), m // tm, I32), one - 1, one, layer,
                  s_gate[:, None], s_up[:, None], s_down[:, None], tm, packed=False)
    return _combine_ln(ys, dest, wsel.T, shared, h, ln_g, ln_b)


def kernel(x, meta, ab_w_in, s5_log_dt, s5_lambda_re, s5_lambda_im, s5_b_re, s5_b_im, s5_c_re, s5_c_im, s5_d, s5_w_glu, gla_w_gate2, gla_b_gate, gla_norm_g, ab_w_out, ret_w_in, ret_norm_g, ret_w_out, ln1_g, ln1_b, ln2_g, ln2_b, moe_router, moe_bias, moe_w_gate, moe_w_up, moe_w_down, shared_w_gate, shared_w_up, shared_w_down):
    bsz, seq, d = x.shape
    length = seq + N_META
    pad = (-length) % CHUNK
    assert pad == CHUNK - N_META, "sequence length must be a multiple of the mixer chunk"
    lp = length + pad
    m = bsz * lp
    h3 = jnp.concatenate([jnp.zeros((bsz, pad, d), x.dtype),
                          jnp.broadcast_to(meta[None].astype(x.dtype), (bsz, N_META, d)), x], axis=1)
    h = h3.reshape(m, d)
    h_bf = h.astype(BF16)
    depth = ln1_g.shape[0]
    for layer in range(depth):
        i = layer // 2
        if layer % 2 == 0:
            s5_w = s5_lambda_re.shape[1] * S5_GROUP
            gla_qk = gla_w_gate2.shape[2]
            gla_w = gla_norm_g.shape[1]
            n_main = s5_w + 2 * gla_qk + 2 * gla_w
            w_in = ab_w_in[i].astype(BF16)
            u = _matmul(h_bf, w_in, 0, s5_w, F32)
            proj = _matmul(h_bf, w_in, s5_w, n_main - s5_w, BF16)
            gate = _gla_gate(h_bf, ab_w_in[i][:, n_main:], gla_w_gate2[i], gla_b_gate[i])
            y_a = _s5_mixer(u, bsz, lp, (s5_log_dt[i], s5_lambda_re[i], s5_lambda_im[i], s5_b_re[i], s5_b_im[i],
                                         s5_c_re[i], s5_c_im[i], s5_d[i], s5_w_glu[i]))
            o = _gla(proj.reshape(bsz, lp, n_main - s5_w), gate.reshape(bsz, lp, gla_qk), gla_norm_g[i],
                     q_off=0, k_off=gla_qk, v_off=2 * gla_qk, r_off=2 * gla_qk + gla_w,
                     dk=gla_qk // GLA_HEADS, dv=gla_w // GLA_HEADS)
            mixed = jnp.concatenate([y_a, o.reshape(m, gla_w)], axis=1)
            w_out = ab_w_out[i].astype(BF16)
        else:
            ret_w = ret_norm_g.shape[1]
            ret_qk = (ret_w_in.shape[2] - 2 * ret_w) // 2
            proj = _matmul(h_bf, ret_w_in[i].astype(BF16), 0, ret_w_in.shape[2], BF16)
            o = _retention(proj.reshape(bsz, lp, ret_w_in.shape[2]), ret_norm_g[i],
                           dk=ret_qk // RET_HEADS, dv=ret_w // RET_HEADS)
            mixed = o.reshape(m, ret_w)
            w_out = ret_w_out[i].astype(BF16)
        h, h_bf = _matmul_res_ln(mixed, w_out, h, ln1_g[layer], ln1_b[layer])
        h, h_bf = _moe_ln(h, h_bf, layer, moe_router[layer], moe_bias[layer], moe_w_gate, moe_w_up, moe_w_down,
                          shared_w_gate, shared_w_up, shared_w_down, ln2_g[layer], ln2_b[layer])
    return h.reshape(bsz, lp, d)[:, pad + N_META:]
```
